```python
import jax
import jax.numpy as jnp
from jax import lax
import numpy as np

D_MODEL = 2048
BATCH = 4
SEQ = 8192
DEPTH = 2

GRID_W = 64
CTX_LEN = 256
N_EVEN = (DEPTH + 1) // 2
N_ODD = DEPTH // 2
EPS = 1e-6
D_FF = 5632
FFN_STEP = 0.5
LRU_WIDTH = D_MODEL // 2
LRU_BLOCKS = 8
LRU_BLOCK = LRU_WIDTH // LRU_BLOCKS
LRU_C = 8.0
CONV_W = 4
CONV_LEFT = 2
RET_HEADS = 4
RET_DK = 256
RET_DV = 256
RET_QK = RET_HEADS * RET_DK
RET_V = RET_HEADS * RET_DV
RET_CHUNK = 128
RET_THETA = 10000.0
EV_IN = 2 * LRU_WIDTH + 2 * RET_QK + 2 * RET_V
EV_MIX = LRU_WIDTH + RET_V
POOL_WINDOWS = (2, 4, 8, 16)
POOL_GROUP = 128
POOL_WIDTH = POOL_GROUP * len(POOL_WINDOWS)
ATT_HEADS = 12
KV_HEADS = 4
GROUP = ATT_HEADS // KV_HEADS
HEAD_DIM = 128
ATT_QW = ATT_HEADS * HEAD_DIM
ATT_KVW = KV_HEADS * HEAD_DIM
Q_BLOCK = 128
ROPE_THETA = 10000.0
OD_IN = POOL_WIDTH + ATT_QW + 2 * ATT_KVW
OD_MIX = POOL_WIDTH + ATT_QW

kernel_name = "hybrid_lru_retention_pool_gqa_diffusion_block"


def rmsnorm(x, g):
    xf = x.astype(jnp.float32)
    y = xf * lax.rsqrt(jnp.mean(xf * xf, axis=-1, keepdims=True) + EPS)
    return (y * g.astype(jnp.float32)).astype(x.dtype)


def modulate(x, g, shift, scale):
    return rmsnorm(x, g) * (1 + scale) + shift


def residual(x, y, g_post, gate, w):
    return x + w * gate * rmsnorm(y, g_post)


def swiglu(h, w_gate, w_up, w_down):
    return (jax.nn.silu(h @ w_gate) * (h @ w_up)) @ w_down


def split_heads(t, h, d):
    return t.reshape(t.shape[0], t.shape[1], h, d)


def apply_rotary(x, cos, sin):
    x1, x2 = jnp.split(x, 2, axis=-1)
    c = cos[:, None, :]
    s = sin[:, None, :]
    return jnp.concatenate([x1 * c - x2 * s, x1 * s + x2 * c], axis=-1).astype(x.dtype)


def dwconv(x, w, b):
    L = x.shape[1]
    xp = jnp.pad(x, ((0, 0), (CONV_LEFT, CONV_W - 1 - CONV_LEFT), (0, 0)))
    y = xp[:, 0:L] * w[0]
    for k in range(1, CONV_W):
        y = y + xp[:, k:k + L] * w[k]
    return y + b


def blockdiag(x, w, b):
    B_, L, _ = x.shape
    xb = x.reshape(B_, L, LRU_BLOCKS, LRU_BLOCK)
    return jnp.einsum('blnc,ncd->blnd', xb, w).reshape(B_, L, LRU_WIDTH) + b


def linear_scan(a, b, h0):
    def comb(l, r):
        return r[0] * l[0], r[0] * l[1] + r[1]
    a_cum, b_cum = lax.associative_scan(comb, (a, b), axis=1)
    if h0 is None:
        return b_cum
    return b_cum + a_cum * h0[:, None, :]


def rglru_coeffs(u, wa, ba, wx, bx, lam):
    r = jax.nn.sigmoid(blockdiag(u, wa, ba))
    i = jax.nn.sigmoid(blockdiag(u, wx, bx))
    log_a = -LRU_C * r * jax.nn.softplus(-lam.astype(jnp.float32))
    a = jnp.exp(log_a)
    bterm = jnp.sqrt(-jnp.expm1(2.0 * log_a)) * (i * u)
    return a, bterm


def rglru_bidir(ul, uc, wa, ba, wx, bx, lam):
    outs_l, outs_c = [], []
    for d in range(2):
        al, bl = rglru_coeffs(ul, wa[d], ba[d], wx[d], bx[d], lam[d])
        ac, bc = rglru_coeffs(uc, wa[d], ba[d], wx[d], bx[d], lam[d])
        if d == 1:
            al, bl, ac, bc = [jnp.flip(t, axis=1) for t in (al, bl, ac, bc)]
        h_c = linear_scan(ac, bc, None)
        h_l = linear_scan(al, bl, h_c[:, -1])
        if d == 1:
            h_c = jnp.flip(h_c, axis=1)
            h_l = jnp.flip(h_l, axis=1)
        outs_l.append(h_l)
        outs_c.append(h_c)
    return outs_l[0] + outs_l[1], outs_c[0] + outs_c[1]


def retention_scan(q, k, v, log_g, s0):
    B_, H, L, _ = q.shape
    C = RET_CHUNK
    n = L // C
    idx = jnp.arange(C, dtype=jnp.float32)
    diff = idx[:, None] - idx[None, :]
    lg = log_g[:, None, None]
    intra = jnp.where(diff >= 0, jnp.exp(lg * jnp.maximum(diff, 0.0)), 0.0)
    q_dec = jnp.exp(log_g[:, None] * (idx + 1.0))
    k_dec = jnp.exp(log_g[:, None] * (C - 1.0 - idx))
    s_dec = jnp.exp(log_g * C)

    def to_chunks(t):
        return t.reshape(B_, H, n, C, t.shape[-1]).transpose(2, 0, 1, 3, 4)

    def step(s, qkv):
        qc, kc, vc = qkv
        scores = jnp.einsum('bhid,bhjd->bhij', qc, kc) * intra
        o = (jnp.einsum('bhij,bhje->bhie', scores, vc)
             + jnp.einsum('bhid,bhde->bhie', qc * q_dec[..., None], s))
        s_new = s * s_dec[:, None, None] + jnp.einsum('bhjd,bhje->bhde', kc * k_dec[..., None], vc)
        return s_new, o

    s_fin, o = lax.scan(step, s0, (to_chunks(q), to_chunks(k), to_chunks(v)))
    o = o.transpose(1, 2, 0, 3, 4).reshape(B_, H, L, o.shape[-1])
    return o, s_fin


def retention_bidir(ql, kl, vl, qc, kc, vc, log_g):
    B_, H = ql.shape[0], ql.shape[1]
    s0 = jnp.zeros((B_, H, RET_DK, RET_DV), jnp.float32)
    outs_l, outs_c = [], []
    for d in range(2):
        seqs = (ql, kl, vl, qc, kc, vc)
        if d == 1:
            seqs = tuple(jnp.flip(t, axis=2) for t in seqs)
        o_c, s_c = retention_scan(seqs[3], seqs[4], seqs[5], log_g[d], s0)
        o_l, _ = retention_scan(seqs[0], seqs[1], seqs[2], log_g[d], s_c)
        if d == 1:
            o_c = jnp.flip(o_c, axis=2)
            o_l = jnp.flip(o_l, axis=2)
        outs_l.append(o_l)
        outs_c.append(o_c)
    return outs_l[0] + outs_l[1], outs_c[0] + outs_c[1]


def head_groupnorm(o, g):
    mu = jnp.mean(o, axis=-1, keepdims=True)
    var = jnp.mean(jnp.square(o - mu), axis=-1, keepdims=True)
    y = (o - mu) * lax.rsqrt(var + EPS)
    B_, H, L, dv = y.shape
    return y.transpose(0, 2, 1, 3).reshape(B_, L, H * dv) * g


def to_bhld(t):
    return t.astype(jnp.float32).transpose(0, 2, 1, 3)


def even_mixer(hl, hc, w_in, w_out, conv_w, conv_b, wa, ba, wx, bx, lam, decay_logit, gn_g, cos, sin):
    splits = [LRU_WIDTH, 2 * LRU_WIDTH, 2 * LRU_WIDTH + RET_QK, 2 * LRU_WIDTH + 2 * RET_QK,
              2 * LRU_WIDTH + 2 * RET_QK + RET_V]
    gl, rl, ql, kl, vl, ol = jnp.split(hl @ w_in, splits, axis=-1)
    gc, rc, qc, kc, vc, oc = jnp.split(hc @ w_in, splits, axis=-1)
    ul = dwconv(rl, conv_w, conv_b).astype(jnp.float32)
    uc = dwconv(rc, conv_w, conv_b).astype(jnp.float32)
    hl_lru, hc_lru = rglru_bidir(ul, uc, wa, ba, wx, bx, lam)
    lru_l = jax.nn.gelu(gl.astype(jnp.float32)) * hl_lru
    lru_c = jax.nn.gelu(gc.astype(jnp.float32)) * hc_lru
    k_scale = RET_DK ** -0.5
    ql = to_bhld(apply_rotary(split_heads(ql, RET_HEADS, RET_DK), cos, sin))
    kl = to_bhld(apply_rotary(split_heads(kl, RET_HEADS, RET_DK), cos, sin)) * k_scale
    vl = to_bhld(split_heads(vl, RET_HEADS, RET_DV))
    qc = to_bhld(split_heads(qc, RET_HEADS, RET_DK))
    kc = to_bhld(split_heads(kc, RET_HEADS, RET_DK)) * k_scale
    vc = to_bhld(split_heads(vc, RET_HEADS, RET_DV))
    log_g = -jax.nn.softplus(-decay_logit.astype(jnp.float32))
    rl_out, rc_out = retention_bidir(ql, kl, vl, qc, kc, vc, log_g)
    ret_l = head_groupnorm(rl_out, gn_g) * jax.nn.silu(ol.astype(jnp.float32))
    ret_c = head_groupnorm(rc_out, gn_g) * jax.nn.silu(oc.astype(jnp.float32))
    yl = jnp.concatenate([lru_l, ret_l], axis=-1).astype(hl.dtype) @ w_out
    yc = jnp.concatenate([lru_c, ret_c], axis=-1).astype(hc.dtype) @ w_out
    return yl, yc


def multiscale_pool(x, pool_w, pool_scale):
    B_, L, _ = x.shape
    xf = x.astype(jnp.float32)
    cs = jnp.concatenate([jnp.zeros((B_, 1, POOL_WIDTH), jnp.float32), jnp.cumsum(xf, axis=1)], axis=1)
    t = jnp.arange(L)
    outs = []
    for gi, w in enumerate(POOL_WINDOWS):
        lo = jnp.clip(t - w // 2, 0, L)
        hi = jnp.clip(t + w // 2, 0, L)
        sl = slice(gi * POOL_GROUP, (gi + 1) * POOL_GROUP)
        csg = cs[..., sl]
        cnt = (hi - lo).astype(jnp.float32)[None, :, None]
        mean = (csg[:, hi] - csg[:, lo]) / cnt
        outs.append(jnp.einsum('blc,cd->bld', mean - xf[..., sl], pool_w[gi].astype(jnp.float32)))
    return (jnp.concatenate(outs, axis=-1) * pool_scale).astype(x.dtype)


def attend(q, k, v):
    B_, Lq = q.shape[0], q.shape[1]
    nb = Lq // Q_BLOCK
    qb = q.reshape(B_, nb, Q_BLOCK, KV_HEADS, GROUP, HEAD_DIM).transpose(1, 0, 3, 4, 2, 5)
    kt = k.transpose(0, 2, 1, 3)
    vt = v.transpose(0, 2, 1, 3)
    scale = HEAD_DIM ** -0.5

    def blk(qi):
        s = jnp.einsum('bkgqd,bksd->bkgqs', qi, kt, preferred_element_type=jnp.float32) * scale
        p = jax.nn.softmax(s, axis=-1)
        return jnp.einsum('bkgqs,bksd->bkgqd', p.astype(vt.dtype), vt)

    o = lax.map(blk, qb)
    return o.transpose(1, 0, 4, 2, 3, 5).reshape(B_, Lq, ATT_QW)


def odd_mixer(hl, hc, w_in, w_out, pool_w, pool_scale, q_g, k_g, cos, sin, with_ctx):
    splits = [POOL_WIDTH, POOL_WIDTH + ATT_QW, POOL_WIDTH + ATT_QW + ATT_KVW]
    pool_l, ql, kl, vl = jnp.split(hl @ w_in, splits, axis=-1)
    ql = apply_rotary(rmsnorm(split_heads(ql, ATT_HEADS, HEAD_DIM), q_g), cos, sin)
    kl = apply_rotary(rmsnorm(split_heads(kl, KV_HEADS, HEAD_DIM), k_g), cos, sin)
    vl = split_heads(vl, KV_HEADS, HEAD_DIM)
    if with_ctx:
        pool_c, qc, kc, vc = jnp.split(hc @ w_in, splits, axis=-1)
    else:
        kc, vc = jnp.split(hc @ w_in[:, POOL_WIDTH + ATT_QW:], [ATT_KVW], axis=-1)
    kc = rmsnorm(split_heads(kc, KV_HEADS, HEAD_DIM), k_g)
    vc = split_heads(vc, KV_HEADS, HEAD_DIM)
    att_l = attend(ql, jnp.concatenate([kc, kl], axis=1), jnp.concatenate([vc, vl], axis=1))
    yl = jnp.concatenate([multiscale_pool(pool_l, pool_w, pool_scale), att_l.astype(hl.dtype)], axis=-1) @ w_out
    if not with_ctx:
        return yl, None
    qc = rmsnorm(split_heads(qc, ATT_HEADS, HEAD_DIM), q_g)
    att_c = attend(qc, kc, vc)
    yc = jnp.concatenate([multiscale_pool(pool_c, pool_w, pool_scale), att_c.astype(hc.dtype)], axis=-1) @ w_out
    return yl, yc


def setup_inputs(seed: int = 0) -> dict:
    key = jax.random.key(seed)
    ks = list(jax.random.split(key, 28))

    def nrm(i, shape, s):
        return jax.random.normal(ks[i], shape, jnp.float32) * s

    D = D_MODEL
    u = jax.random.uniform(ks[19], (N_EVEN, 2, LRU_WIDTH), jnp.float32, 0.9, 0.999)
    a = u ** (1.0 / LRU_C)
    lam = jnp.log(a) - jnp.log1p(-a)
    gam = 1.0 - 2.0 ** (-5.0 - jnp.arange(RET_HEADS, dtype=jnp.float32))
    decay_base = jnp.log(gam) - jnp.log1p(-gam)
    return {
        "x": nrm(0, (BATCH, SEQ, D), 1.0),
        "c": nrm(1, (BATCH, D), 1.0),
        "ctx": nrm(2, (BATCH, CTX_LEN, D), 1.0),
        "c_ctx": nrm(3, (D,), 1.0),
        "mod_w": nrm(4, (DEPTH, D, 9 * D), 0.5 * D ** -0.5),
        "mod_b": nrm(5, (DEPTH, 9 * D), 0.02),
        "norm_pre": 1.0 + nrm(6, (DEPTH, 3, D), 0.02),
        "norm_post": 1.0 + nrm(7, (DEPTH, 3, D), 0.02),
        "ffn_gate": nrm(8, (DEPTH, 2, D, D_FF), D ** -0.5),
        "ffn_up": nrm(9, (DEPTH, 2, D, D_FF), D ** -0.5),
        "ffn_down": nrm(10, (DEPTH, 2, D_FF, D), D_FF ** -0.5),
        "ev_w_in": nrm(11, (N_EVEN, D, EV_IN), D ** -0.5),
        "ev_w_out": nrm(12, (N_EVEN, EV_MIX, D), EV_MIX ** -0.5),
        "lru_conv_w": nrm(13, (N_EVEN, CONV_W, LRU_WIDTH), CONV_W ** -0.5),
        "lru_conv_b": nrm(14, (N_EVEN, LRU_WIDTH), 0.01),
        "lru_wa": nrm(15, (N_EVEN, 2, LRU_BLOCKS, LRU_BLOCK, LRU_BLOCK), LRU_BLOCK ** -0.5),
        "lru_ba": nrm(16, (N_EVEN, 2, LRU_WIDTH), 0.01),
        "lru_wx": nrm(17, (N_EVEN, 2, LRU_BLOCKS, LRU_BLOCK, LRU_BLOCK), LRU_BLOCK ** -0.5),
        "lru_bx": nrm(18, (N_EVEN, 2, LRU_WIDTH), 0.01),
        "lru_lambda": lam,
        "ret_decay_logit": decay_base + nrm(20, (N_EVEN, 2, RET_HEADS), 0.05),
        "ret_gn": 1.0 + nrm(21, (N_EVEN, RET_V), 0.02),
        "od_w_in": nrm(22, (N_ODD, D, OD_IN), D ** -0.5),
        "od_w_out": nrm(23, (N_ODD, OD_MIX, D), OD_MIX ** -0.5),
        "pool_w": nrm(24, (N_ODD, len(POOL_WINDOWS), POOL_GROUP, POOL_GROUP), POOL_GROUP ** -0.5),
        "pool_scale": 1.0 + nrm(25, (N_ODD, POOL_WIDTH), 0.1),
        "q_norm": 1.0 + nrm(26, (N_ODD, HEAD_DIM), 0.02),
        "k_norm": 1.0 + nrm(27, (N_ODD, HEAD_DIM), 0.02),
    }


def reference(x, c, ctx, c_ctx, mod_w, mod_b, norm_pre, norm_post, ffn_gate, ffn_up, ffn_down,
              ev_w_in, ev_w_out, lru_conv_w, lru_conv_b, lru_wa, lru_ba, lru_wx, lru_bx, lru_lambda,
              ret_decay_logit, ret_gn, od_w_in, od_w_out, pool_w, pool_scale, q_norm, k_norm):
    D = D_MODEL
    S = x.shape[1]
    rows = S // GRID_W
    row = jnp.repeat(jnp.arange(rows, dtype=jnp.float32), GRID_W)
    col = jnp.tile(jnp.arange(GRID_W, dtype=jnp.float32), rows)
    n_ax = HEAD_DIM // 4
    f_ax = ROPE_THETA ** (-jnp.arange(n_ax, dtype=jnp.float32) / n_ax)
    ang2 = jnp.concatenate([row[:, None] * f_ax, col[:, None] * f_ax], axis=-1)
    cos2, sin2 = jnp.cos(ang2), jnp.sin(ang2)
    n_r = RET_DK // 2
    f_r = RET_THETA ** (-jnp.arange(n_r, dtype=jnp.float32) / n_r)
    ang1 = jnp.arange(S, dtype=jnp.float32)[:, None] * f_r
    cos1, sin1 = jnp.cos(ang1), jnp.sin(ang1)

    sc = jax.nn.silu(c)
    scc = jax.nn.silu(c_ctx)
    xl, xc = x, ctx
    for li in range(DEPTH):
        last = li == DEPTH - 1
        n_ctx_sub = 2 if last else 3
        mod_l = (sc @ mod_w[li] + mod_b[li]).reshape(-1, 3, 3, 1, D)
        mod_c = (scc @ mod_w[li][:, :n_ctx_sub * 3 * D] + mod_b[li, :n_ctx_sub * 3 * D]).reshape(n_ctx_sub, 3, D)

        hl = modulate(xl, norm_pre[li, 0], mod_l[:, 0, 0], mod_l[:, 0, 1])
        hc = modulate(xc, norm_pre[li, 0], mod_c[0, 0], mod_c[0, 1])
        xl = residual(xl, swiglu(hl, ffn_gate[li, 0], ffn_up[li, 0], ffn_down[li, 0]), norm_post[li, 0], mod_l[:, 0, 2], FFN_STEP)
        xc = residual(xc, swiglu(hc, ffn_gate[li, 0], ffn_up[li, 0], ffn_down[li, 0]), norm_post[li, 0], mod_c[0, 2], FFN_STEP)

        hl = modulate(xl, norm_pre[li, 1], mod_l[:, 1, 0], mod_l[:, 1, 1])
        hc = modulate(xc, norm_pre[li, 1], mod_c[1, 0], mod_c[1, 1])
        if li % 2 == 0:
            e = li // 2
            yl, yc = even_mixer(hl, hc, ev_w_in[e], ev_w_out[e], lru_conv_w[e], lru_conv_b[e],
                                lru_wa[e], lru_ba[e], lru_wx[e], lru_bx[e], lru_lambda[e],
                                ret_decay_logit[e], ret_gn[e], cos1, sin1)
        else:
            o = li // 2
            yl, yc = odd_mixer(hl, hc, od_w_in[o], od_w_out[o], pool_w[o], pool_scale[o],
                               q_norm[o], k_norm[o], cos2, sin2, not last)
        xl = residual(xl, yl, norm_post[li, 1], mod_l[:, 1, 2], 1.0)
        if not last:
            xc = residual(xc, yc, norm_post[li, 1], mod_c[1, 2], 1.0)

        hl = modulate(xl, norm_pre[li, 2], mod_l[:, 2, 0], mod_l[:, 2, 1])
        xl = residual(xl, swiglu(hl, ffn_gate[li, 1], ffn_up[li, 1], ffn_down[li, 1]), norm_post[li, 2], mod_l[:, 2, 2], FFN_STEP)
        if not last:
            hc = modulate(xc, norm_pre[li, 2], mod_c[2, 0], mod_c[2, 1])
            xc = residual(xc, swiglu(hc, ffn_gate[li, 1], ffn_up[li, 1], ffn_down[li, 1]), norm_post[li, 2], mod_c[2, 2], FFN_STEP)
    return xl
```

```python
import functools

import jax
import jax.numpy as jnp
from jax import lax
from jax.experimental import pallas as pl
from jax.experimental.pallas import tpu as pltpu

D_MODEL = 2048
DEPTH = 2
GRID_W = 64
EPS = 1e-6
D_FF = 5632
FFN_STEP = 0.5
LRU_WIDTH = D_MODEL // 2
LRU_BLOCKS = 8
LRU_BLOCK = LRU_WIDTH // LRU_BLOCKS
LRU_C = 8.0
CONV_W = 4
CONV_LEFT = 2
RET_HEADS = 4
RET_DK = 256
RET_DV = 256
RET_QK = RET_HEADS * RET_DK
RET_V = RET_HEADS * RET_DV
RET_CHUNK = 128
RET_THETA = 10000.0
POOL_WINDOWS = (2, 4, 8, 16)
POOL_GROUP = 128
POOL_WIDTH = POOL_GROUP * len(POOL_WINDOWS)
ATT_HEADS = 12
KV_HEADS = 4
GROUP = ATT_HEADS // KV_HEADS
HEAD_DIM = 128
ATT_QW = ATT_HEADS * HEAD_DIM
ATT_KVW = KV_HEADS * HEAD_DIM
Q_BLOCK = 128
ROPE_THETA = 10000.0

VMEM_LIMIT_BYTES = 56 * 1024 * 1024

FFN_ROW_TILE = 512
FFN_FF_TILE = 512


def _ffn_kernel(x_ref, mod_ref, gpre_ref, gpost_ref, wg_ref, wu_ref, wd_ref, o_ref, h_ref, acc_ref, *, step):
    j = pl.program_id(1)

    @pl.when(j == 0)
    def _():
        x = x_ref[...]
        ms = jnp.mean(x * x, axis=-1, keepdims=True)
        y = x * lax.rsqrt(ms + EPS) * gpre_ref[...]
        h = y * (1.0 + mod_ref[0, 1:2, :]) + mod_ref[0, 0:1, :]
        h_ref[...] = h.astype(jnp.bfloat16)
        acc_ref[...] = jnp.zeros_like(acc_ref)

    h = h_ref[...]
    g = jnp.dot(h, wg_ref[...], preferred_element_type=jnp.float32)
    u = jnp.dot(h, wu_ref[...], preferred_element_type=jnp.float32)
    a = (g * jax.nn.sigmoid(g)) * u
    acc_ref[...] += jnp.dot(a.astype(jnp.bfloat16), wd_ref[...], preferred_element_type=jnp.float32)

    @pl.when(j == pl.num_programs(1) - 1)
    def _():
        y = acc_ref[...]
        ms = jnp.mean(y * y, axis=-1, keepdims=True)
        yn = y * lax.rsqrt(ms + EPS) * gpost_ref[...]
        o_ref[...] = x_ref[...] + (step * mod_ref[0, 2:3, :]) * yn


def ffn_sublayer(x, mod, g_pre, g_post, w_gate, w_up, w_down, rows_per_group):
    rows, d = x.shape
    f = w_gate.shape[1]
    tm, tf = FFN_ROW_TILE, FFN_FF_TILE
    assert rows % tm == 0 and f % tf == 0 and rows_per_group % tm == 0
    tiles_per_group = rows_per_group // tm
    last_group = mod.shape[0] - 1
    return pl.pallas_call(
        functools.partial(_ffn_kernel, step=FFN_STEP),
        grid=(rows // tm, f // tf),
        in_specs=[
            pl.BlockSpec((tm, d), lambda i, j: (i, 0)),
            pl.BlockSpec((1, 3, d), lambda i, j: (jnp.minimum(i // tiles_per_group, last_group), 0, 0)),
            pl.BlockSpec((1, d), lambda i, j: (0, 0)),
            pl.BlockSpec((1, d), lambda i, j: (0, 0)),
            pl.BlockSpec((d, tf), lambda i, j: (0, j)),
            pl.BlockSpec((d, tf), lambda i, j: (0, j)),
            pl.BlockSpec((tf, d), lambda i, j: (j, 0)),
        ],
        out_specs=pl.BlockSpec((tm, d), lambda i, j: (i, 0)),
        out_shape=jax.ShapeDtypeStruct((rows, d), jnp.float32),
        scratch_shapes=[pltpu.VMEM((tm, d), jnp.bfloat16), pltpu.VMEM((tm, d), jnp.float32)],
        compiler_params=pltpu.CompilerParams(
            dimension_semantics=("arbitrary", "arbitrary"), vmem_limit_bytes=VMEM_LIMIT_BYTES),
        name="ffn_sublayer",
    )(x, mod, g_pre.reshape(1, d), g_post.reshape(1, d), w_gate, w_up, w_down)


def _rmsnorm(x, g):
    xf = x.astype(jnp.float32)
    y = xf * lax.rsqrt(jnp.mean(xf * xf, axis=-1, keepdims=True) + EPS)
    return (y * g.astype(jnp.float32)).astype(x.dtype)


def _modulate(x, g, shift, scale):
    return _rmsnorm(x, g) * (1 + scale) + shift


def _residual(x, y, g_post, gate, w):
    return x + w * gate * _rmsnorm(y, g_post)


def _split_heads(t, h, d):
    return t.reshape(t.shape[0], t.shape[1], h, d)


def _apply_rotary(x, cos, sin):
    x1, x2 = jnp.split(x, 2, axis=-1)
    c = cos[:, None, :]
    s = sin[:, None, :]
    return jnp.concatenate([x1 * c - x2 * s, x1 * s + x2 * c], axis=-1).astype(x.dtype)


def _dwconv(x, w, b):
    L = x.shape[1]
    xp = jnp.pad(x, ((0, 0), (CONV_LEFT, CONV_W - 1 - CONV_LEFT), (0, 0)))
    y = xp[:, 0:L] * w[0]
    for k in range(1, CONV_W):
        y = y + xp[:, k:k + L] * w[k]
    return y + b


def _blockdiag(x, w, b):
    B_, L, _ = x.shape
    xb = x.reshape(B_, L, LRU_BLOCKS, LRU_BLOCK)
    return jnp.einsum('blnc,ncd->blnd', xb, w).reshape(B_, L, LRU_WIDTH) + b


def _linear_scan(a, b, h0):
    def comb(l, r):
        return r[0] * l[0], r[0] * l[1] + r[1]
    a_cum, b_cum = lax.associative_scan(comb, (a, b), axis=1)
    if h0 is None:
        return b_cum
    return b_cum + a_cum * h0[:, None, :]


def _rglru_coeffs(u, wa, ba, wx, bx, lam):
    r = jax.nn.sigmoid(_blockdiag(u, wa, ba))
    i = jax.nn.sigmoid(_blockdiag(u, wx, bx))
    log_a = -LRU_C * r * jax.nn.softplus(-lam.astype(jnp.float32))
    a = jnp.exp(log_a)
    bterm = jnp.sqrt(-jnp.expm1(2.0 * log_a)) * (i * u)
    return a, bterm


def _rglru_bidir(ul, uc, wa, ba, wx, bx, lam):
    outs_l, outs_c = [], []
    for d in range(2):
        al, bl = _rglru_coeffs(ul, wa[d], ba[d], wx[d], bx[d], lam[d])
        ac, bc = _rglru_coeffs(uc, wa[d], ba[d], wx[d], bx[d], lam[d])
        if d == 1:
            al, bl, ac, bc = [jnp.flip(t, axis=1) for t in (al, bl, ac, bc)]
        h_c = _linear_scan(ac, bc, None)
        h_l = _linear_scan(al, bl, h_c[:, -1])
        if d == 1:
            h_c = jnp.flip(h_c, axis=1)
            h_l = jnp.flip(h_l, axis=1)
        outs_l.append(h_l)
        outs_c.append(h_c)
    return outs_l[0] + outs_l[1], outs_c[0] + outs_c[1]


def _retention_scan(q, k, v, log_g, s0):
    B_, H, L, _ = q.shape
    C = RET_CHUNK
    n = L // C
    idx = jnp.arange(C, dtype=jnp.float32)
    diff = idx[:, None] - idx[None, :]
    lg = log_g[:, None, None]
    intra = jnp.where(diff >= 0, jnp.exp(lg * jnp.maximum(diff, 0.0)), 0.0)
    q_dec = jnp.exp(log_g[:, None] * (idx + 1.0))
    k_dec = jnp.exp(log_g[:, None] * (C - 1.0 - idx))
    s_dec = jnp.exp(log_g * C)

    def to_chunks(t):
        return t.reshape(B_, H, n, C, t.shape[-1]).transpose(2, 0, 1, 3, 4)

    def step(s, qkv):
        qc, kc, vc = qkv
        scores = jnp.einsum('bhid,bhjd->bhij', qc, kc) * intra
        o = (jnp.einsum('bhij,bhje->bhie', scores, vc)
             + jnp.einsum('bhid,bhde->bhie', qc * q_dec[..., None], s))
        s_new = s * s_dec[:, None, None] + jnp.einsum('bhjd,bhje->bhde', kc * k_dec[..., None], vc)
        return s_new, o

    s_fin, o = lax.scan(step, s0, (to_chunks(q), to_chunks(k), to_chunks(v)))
    o = o.transpose(1, 2, 0, 3, 4).reshape(B_, H, L, o.shape[-1])
    return o, s_fin


def _retention_bidir(ql, kl, vl, qc, kc, vc, log_g):
    B_, H = ql.shape[0], ql.shape[1]
    s0 = jnp.zeros((B_, H, RET_DK, RET_DV), jnp.float32)
    outs_l, outs_c = [], []
    for d in range(2):
        seqs = (ql, kl, vl, qc, kc, vc)
        if d == 1:
            seqs = tuple(jnp.flip(t, axis=2) for t in seqs)
        o_c, s_c = _retention_scan(seqs[3], seqs[4], seqs[5], log_g[d], s0)
        o_l, _ = _retention_scan(seqs[0], seqs[1], seqs[2], log_g[d], s_c)
        if d == 1:
            o_c = jnp.flip(o_c, axis=2)
            o_l = jnp.flip(o_l, axis=2)
        outs_l.append(o_l)
        outs_c.append(o_c)
    return outs_l[0] + outs_l[1], outs_c[0] + outs_c[1]


def _head_groupnorm(o, g):
    mu = jnp.mean(o, axis=-1, keepdims=True)
    var = jnp.mean(jnp.square(o - mu), axis=-1, keepdims=True)
    y = (o - mu) * lax.rsqrt(var + EPS)
    B_, H, L, dv = y.shape
    return y.transpose(0, 2, 1, 3).reshape(B_, L, H * dv) * g


def _to_bhld(t):
    return t.astype(jnp.float32).transpose(0, 2, 1, 3)


def _even_mixer(hl, hc, w_in, w_out, conv_w, conv_b, wa, ba, wx, bx, lam, decay_logit, gn_g, cos, sin):
    splits = [LRU_WIDTH, 2 * LRU_WIDTH, 2 * LRU_WIDTH + RET_QK, 2 * LRU_WIDTH + 2 * RET_QK,
              2 * LRU_WIDTH + 2 * RET_QK + RET_V]
    gl, rl, ql, kl, vl, ol = jnp.split(hl @ w_in, splits, axis=-1)
    gc, rc, qc, kc, vc, oc = jnp.split(hc @ w_in, splits, axis=-1)
    ul = _dwconv(rl, conv_w, conv_b).astype(jnp.float32)
    uc = _dwconv(rc, conv_w, conv_b).astype(jnp.float32)
    hl_lru, hc_lru = _rglru_bidir(ul, uc, wa, ba, wx, bx, lam)
    lru_l = jax.nn.gelu(gl.astype(jnp.float32)) * hl_lru
    lru_c = jax.nn.gelu(gc.astype(jnp.float32)) * hc_lru
    k_scale = RET_DK ** -0.5
    ql = _to_bhld(_apply_rotary(_split_heads(ql, RET_HEADS, RET_DK), cos, sin))
    kl = _to_bhld(_apply_rotary(_split_heads(kl, RET_HEADS, RET_DK), cos, sin)) * k_scale
    vl = _to_bhld(_split_heads(vl, RET_HEADS, RET_DV))
    qc = _to_bhld(_split_heads(qc, RET_HEADS, RET_DK))
    kc = _to_bhld(_split_heads(kc, RET_HEADS, RET_DK)) * k_scale
    vc = _to_bhld(_split_heads(vc, RET_HEADS, RET_DV))
    log_g = -jax.nn.softplus(-decay_logit.astype(jnp.float32))
    rl_out, rc_out = _retention_bidir(ql, kl, vl, qc, kc, vc, log_g)
    ret_l = _head_groupnorm(rl_out, gn_g) * jax.nn.silu(ol.astype(jnp.float32))
    ret_c = _head_groupnorm(rc_out, gn_g) * jax.nn.silu(oc.astype(jnp.float32))
    yl = jnp.concatenate([lru_l, ret_l], axis=-1).astype(hl.dtype) @ w_out
    yc = jnp.concatenate([lru_c, ret_c], axis=-1).astype(hc.dtype) @ w_out
    return yl, yc


def _multiscale_pool(x, pool_w, pool_scale):
    B_, L, _ = x.shape
    xf = x.astype(jnp.float32)
    cs = jnp.concatenate([jnp.zeros((B_, 1, POOL_WIDTH), jnp.float32), jnp.cumsum(xf, axis=1)], axis=1)
    t = jnp.arange(L)
    outs = []
    for gi, w in enumerate(POOL_WINDOWS):
        lo = jnp.clip(t - w // 2, 0, L)
        hi = jnp.clip(t + w // 2, 0, L)
        sl = slice(gi * POOL_GROUP, (gi + 1) * POOL_GROUP)
        csg = cs[..., sl]
        cnt = (hi - lo).astype(jnp.float32)[None, :, None]
        mean = (csg[:, hi] - csg[:, lo]) / cnt
        outs.append(jnp.einsum('blc,cd->bld', mean - xf[..., sl], pool_w[gi].astype(jnp.float32)))
    return (jnp.concatenate(outs, axis=-1) * pool_scale).astype(x.dtype)


def _attend(q, k, v):
    B_, Lq = q.shape[0], q.shape[1]
    nb = Lq // Q_BLOCK
    qb = q.reshape(B_, nb, Q_BLOCK, KV_HEADS, GROUP, HEAD_DIM).transpose(1, 0, 3, 4, 2, 5)
    kt = k.transpose(0, 2, 1, 3)
    vt = v.transpose(0, 2, 1, 3)
    scale = HEAD_DIM ** -0.5

    def blk(qi):
        s = jnp.einsum('bkgqd,bksd->bkgqs', qi, kt, preferred_element_type=jnp.float32) * scale
        p = jax.nn.softmax(s, axis=-1)
        return jnp.einsum('bkgqs,bksd->bkgqd', p.astype(vt.dtype), vt)

    o = lax.map(blk, qb)
    return o.transpose(1, 0, 4, 2, 3, 5).reshape(B_, Lq, ATT_QW)


def _odd_mixer(hl, hc, w_in, w_out, pool_w, pool_scale, q_g, k_g, cos, sin, with_ctx):
    splits = [POOL_WIDTH, POOL_WIDTH + ATT_QW, POOL_WIDTH + ATT_QW + ATT_KVW]
    pool_l, ql, kl, vl = jnp.split(hl @ w_in, splits, axis=-1)
    ql = _apply_rotary(_rmsnorm(_split_heads(ql, ATT_HEADS, HEAD_DIM), q_g), cos, sin)
    kl = _apply_rotary(_rmsnorm(_split_heads(kl, KV_HEADS, HEAD_DIM), k_g), cos, sin)
    vl = _split_heads(vl, KV_HEADS, HEAD_DIM)
    if with_ctx:
        pool_c, qc, kc, vc = jnp.split(hc @ w_in, splits, axis=-1)
    else:
        kc, vc = jnp.split(hc @ w_in[:, POOL_WIDTH + ATT_QW:], [ATT_KVW], axis=-1)
    kc = _rmsnorm(_split_heads(kc, KV_HEADS, HEAD_DIM), k_g)
    vc = _split_heads(vc, KV_HEADS, HEAD_DIM)
    att_l = _attend(ql, jnp.concatenate([kc, kl], axis=1), jnp.concatenate([vc, vl], axis=1))
    yl = jnp.concatenate([_multiscale_pool(pool_l, pool_w, pool_scale), att_l.astype(hl.dtype)], axis=-1) @ w_out
    if not with_ctx:
        return yl, None
    qc = _rmsnorm(_split_heads(qc, ATT_HEADS, HEAD_DIM), q_g)
    att_c = _attend(qc, kc, vc)
    yc = jnp.concatenate([_multiscale_pool(pool_c, pool_w, pool_scale), att_c.astype(hc.dtype)], axis=-1) @ w_out
    return yl, yc


def kernel(x, c, ctx, c_ctx, mod_w, mod_b, norm_pre, norm_post, ffn_gate, ffn_up, ffn_down,
           ev_w_in, ev_w_out, lru_conv_w, lru_conv_b, lru_wa, lru_ba, lru_wx, lru_bx, lru_lambda,
           ret_decay_logit, ret_gn, od_w_in, od_w_out, pool_w, pool_scale, q_norm, k_norm):
    D = D_MODEL
    B, S, _ = x.shape
    Lc = ctx.shape[1]
    rows = S // GRID_W
    row = jnp.repeat(jnp.arange(rows, dtype=jnp.float32), GRID_W)
    col = jnp.tile(jnp.arange(GRID_W, dtype=jnp.float32), rows)
    n_ax = HEAD_DIM // 4
    f_ax = ROPE_THETA ** (-jnp.arange(n_ax, dtype=jnp.float32) / n_ax)
    ang2 = jnp.concatenate([row[:, None] * f_ax, col[:, None] * f_ax], axis=-1)
    cos2, sin2 = jnp.cos(ang2), jnp.sin(ang2)
    n_r = RET_DK // 2
    f_r = RET_THETA ** (-jnp.arange(n_r, dtype=jnp.float32) / n_r)
    ang1 = jnp.arange(S, dtype=jnp.float32)[:, None] * f_r
    cos1, sin1 = jnp.cos(ang1), jnp.sin(ang1)

    bf = jnp.bfloat16
    wg_bf, wu_bf, wd_bf = ffn_gate.astype(bf), ffn_up.astype(bf), ffn_down.astype(bf)

    sc = jax.nn.silu(c)
    scc = jax.nn.silu(c_ctx)
    xl, xc = x, ctx

    def ffn(xs, mod, li, sub, fi):
        b_, l_, _ = xs.shape
        y = ffn_sublayer(xs.reshape(b_ * l_, D), mod, norm_pre[li, sub], norm_post[li, sub],
                         wg_bf[li, fi], wu_bf[li, fi], wd_bf[li, fi], l_ if mod.shape[0] > 1 else b_ * l_)
        return y.reshape(b_, l_, D)

    for li in range(DEPTH):
        last = li == DEPTH - 1
        n_ctx_sub = 2 if last else 3
        mod_l4 = (sc @ mod_w[li] + mod_b[li]).reshape(-1, 3, 3, D)
        mod_l = mod_l4[:, :, :, None, :]
        mod_c = (scc @ mod_w[li][:, :n_ctx_sub * 3 * D] + mod_b[li, :n_ctx_sub * 3 * D]).reshape(n_ctx_sub, 3, D)

        xl = ffn(xl, mod_l4[:, 0], li, 0, 0)
        xc = ffn(xc, mod_c[0][None], li, 0, 0)

        hl = _modulate(xl, norm_pre[li, 1], mod_l[:, 1, 0], mod_l[:, 1, 1])
        hc = _modulate(xc, norm_pre[li, 1], mod_c[1, 0], mod_c[1, 1])
        if li % 2 == 0:
            e = li // 2
            yl, yc = _even_mixer(hl, hc, ev_w_in[e], ev_w_out[e], lru_conv_w[e], lru_conv_b[e],
                                 lru_wa[e], lru_ba[e], lru_wx[e], lru_bx[e], lru_lambda[e],
                                 ret_decay_logit[e], ret_gn[e], cos1, sin1)
        else:
            o = li // 2
            yl, yc = _odd_mixer(hl, hc, od_w_in[o], od_w_out[o], pool_w[o], pool_scale[o],
                                q_norm[o], k_norm[o], cos2, sin2, not last)
        xl = _residual(xl, yl, norm_post[li, 1], mod_l[:, 1, 2], 1.0)
        if not last:
            xc = _residual(xc, yc, norm_post[li, 1], mod_c[1, 2], 1.0)

        xl = ffn(xl, mod_l4[:, 2], li, 2, 1)
        if not last:
            xc = ffn(xc, mod_c[2][None], li, 2, 1)
    return xl
```

```python
import functools

import jax
import jax.numpy as jnp
from jax import lax
from jax.experimental import pallas as pl
from jax.experimental.pallas import tpu as pltpu

GRID_W = 64
EPS = 1e-6
FFN_STEP = 0.5
LRU_BLOCKS = 8
LRU_C = 8.0
CONV_W = 4
CONV_LEFT = 2
RET_HEADS = 4
RET_CHUNK = 128
RET_THETA = 10000.0
POOL_WINDOWS = (2, 4, 8, 16)
POOL_GROUP = 128
ATT_HEADS = 12
KV_HEADS = 4
GROUP = ATT_HEADS // KV_HEADS
HEAD_DIM = 128
ROPE_THETA = 10000.0

VMEM_LIMIT_BYTES = 56 * 1024 * 1024
SUBLANES = 8

ROW_TILE = 512
OUT_ROW_TILE = 256
FFN_FF_TILE = 512
PROJ_COL_TILE = 1024
SEQ_BLOCK = 256
ATT_Q_TILE = 256
ATT_KV_TILE = 1408
POOL_BLOCK = 512

BF16 = jnp.bfloat16
F32 = jnp.float32


def _params(*sem):
    return pltpu.CompilerParams(dimension_semantics=sem, vmem_limit_bytes=VMEM_LIMIT_BYTES)


def _dot(a, b):
    return jnp.dot(a, b, preferred_element_type=F32)


def _modulated(x, g_pre, mod_ref):
    ms = jnp.mean(x * x, axis=-1, keepdims=True)
    y = x * lax.rsqrt(ms + EPS) * g_pre
    return y * (1.0 + mod_ref[0, 1:2, :]) + mod_ref[0, 0:1, :]


def _gated_residual(x, y, g_post, mod_ref, step):
    ms = jnp.mean(y * y, axis=-1, keepdims=True)
    yn = y * lax.rsqrt(ms + EPS) * g_post
    return x + (step * mod_ref[0, 2:3, :]) * yn


def _mod_spec(d, tm, seq_len, n_groups):
    return pl.BlockSpec((1, 3, d), lambda i, *_: (jnp.minimum((i * tm) // seq_len, n_groups - 1), 0, 0))


def _mod_kernel(c_ref, w_ref, b_ref, o_ref):
    c = c_ref[...]
    sc = c * jax.nn.sigmoid(c)
    o_ref[0] = _dot(sc.astype(BF16), w_ref[0].astype(BF16)) + b_ref[0]


def modulation_vectors(c_all, mod_w, mod_b):
    depth, d, n = mod_w.shape
    tn = 1024
    return pl.pallas_call(
        _mod_kernel,
        grid=(depth, n // tn),
        in_specs=[
            pl.BlockSpec((SUBLANES, d), lambda l, j: (0, 0)),
            pl.BlockSpec((1, d, tn), lambda l, j: (l, 0, j)),
            pl.BlockSpec((1, 1, tn), lambda l, j: (l, 0, j)),
        ],
        out_specs=pl.BlockSpec((1, SUBLANES, tn), lambda l, j: (l, 0, j)),
        out_shape=jax.ShapeDtypeStruct((depth, SUBLANES, n), F32),
        compiler_params=_params("arbitrary", "arbitrary"),
        name="modulation_vectors",
    )(c_all, mod_w, mod_b.reshape(depth, 1, n))


def _ffn_kernel(x_ref, mod_ref, gpre_ref, gpost_ref, wg_ref, wu_ref, wd_ref, o_ref, h_ref, acc_ref):
    j = pl.program_id(1)

    @pl.when(j == 0)
    def _():
        h_ref[...] = _modulated(x_ref[...], gpre_ref[...], mod_ref).astype(BF16)
        acc_ref[...] = jnp.zeros_like(acc_ref)

    h = h_ref[...]
    g = _dot(h, wg_ref[...])
    u = _dot(h, wu_ref[...])
    a = (g * jax.nn.sigmoid(g)) * u
    acc_ref[...] += _dot(a.astype(BF16), wd_ref[...])

    @pl.when(j == pl.num_programs(1) - 1)
    def _():
        o_ref[...] = _gated_residual(x_ref[...], acc_ref[...], gpost_ref[...], mod_ref, FFN_STEP)


def ffn_sublayer(x, n_rows, mod, g_pre, g_post, w_gate, w_up, w_down, seq_len):
    d = x.shape[1]
    f = w_gate.shape[1]
    tm, tf = ROW_TILE, FFN_FF_TILE
    assert n_rows % tm == 0 and f % tf == 0 and seq_len % tm == 0
    return pl.pallas_call(
        _ffn_kernel,
        grid=(n_rows // tm, f // tf),
        in_specs=[
            pl.BlockSpec((tm, d), lambda i, j: (i, 0)),
            _mod_spec(d, tm, seq_len, mod.shape[0]),
            pl.BlockSpec((1, d), lambda i, j: (0, 0)),
            pl.BlockSpec((1, d), lambda i, j: (0, 0)),
            pl.BlockSpec((d, tf), lambda i, j: (0, j)),
            pl.BlockSpec((d, tf), lambda i, j: (0, j)),
            pl.BlockSpec((tf, d), lambda i, j: (j, 0)),
        ],
        out_specs=pl.BlockSpec((tm, d), lambda i, j: (i, 0)),
        out_shape=jax.ShapeDtypeStruct((n_rows, d), F32),
        scratch_shapes=[pltpu.VMEM((tm, d), BF16), pltpu.VMEM((tm, d), F32)],
        compiler_params=_params("arbitrary", "arbitrary"),
        name="ffn_sublayer",
    )(x, mod, g_pre.reshape(1, d), g_post.reshape(1, d), w_gate, w_up, w_down)


def _in_proj_kernel(x_ref, mod_ref, gpre_ref, w_ref, o_ref, h_ref):
    @pl.when(pl.program_id(1) == 0)
    def _():
        h_ref[...] = _modulated(x_ref[...], gpre_ref[...], mod_ref).astype(BF16)

    o_ref[...] = _dot(h_ref[...], w_ref[...])


def mixer_in_proj(x, mod, g_pre, w_in, seq_len):
    rows, d = x.shape
    n = w_in.shape[1]
    tm, tn = ROW_TILE, PROJ_COL_TILE
    assert rows % tm == 0 and n % tn == 0
    return pl.pallas_call(
        _in_proj_kernel,
        grid=(rows // tm, n // tn),
        in_specs=[
            pl.BlockSpec((tm, d), lambda i, j: (i, 0)),
            _mod_spec(d, tm, seq_len, mod.shape[0]),
            pl.BlockSpec((1, d), lambda i, j: (0, 0)),
            pl.BlockSpec((d, tn), lambda i, j: (0, j)),
        ],
        out_specs=pl.BlockSpec((tm, tn), lambda i, j: (i, j)),
        out_shape=jax.ShapeDtypeStruct((rows, n), F32),
        scratch_shapes=[pltpu.VMEM((tm, d), BF16)],
        compiler_params=_params("arbitrary", "arbitrary"),
        name="mixer_in_proj",
    )(x, mod, g_pre.reshape(1, d), w_in)


def _lru_kernel(*refs, reverse, n_lat, final):
    if final:
        (r_ref, rp_ref, rn_ref, cw_ref, cb_ref, wa_ref, ba_ref, wx_ref, bx_ref, lam_ref,
         hf_ref, g_ref, o_ref, a_s, b_s, h_s, hs_s) = refs
    else:
        (r_ref, rp_ref, rn_ref, cw_ref, cb_ref, wa_ref, ba_ref, wx_ref, bx_ref, lam_ref,
         o_ref, a_s, b_s, h_s) = refs
        hs_s = o_ref
    s = pl.program_id(1)
    tb, w = r_ref.shape
    is_lat = s > 0
    k = (n_lat - s) if reverse else (s - 1)
    has_prev = jnp.logical_and(is_lat, k > 0)
    has_next = jnp.logical_and(is_lat, k < n_lat - 1)

    prev = jnp.where(has_prev, rp_ref[...], 0.0)
    nxt = jnp.where(has_next, rn_ref[...], 0.0)
    ext = jnp.concatenate([prev, r_ref[...], nxt], axis=0)
    base = SUBLANES - CONV_LEFT
    u = ext[base:base + tb] * cw_ref[0:1, :]
    for t in range(1, CONV_W):
        u = u + ext[base + t:base + t + tb] * cw_ref[t:t + 1, :]
    u = u + cb_ref[...]

    ub = u.astype(BF16)
    blk = w // LRU_BLOCKS
    ra = jnp.concatenate([_dot(ub[:, n * blk:(n + 1) * blk], wa_ref[n]) for n in range(LRU_BLOCKS)], axis=1)
    xa = jnp.concatenate([_dot(ub[:, n * blk:(n + 1) * blk], wx_ref[n]) for n in range(LRU_BLOCKS)], axis=1)
    r_gate = jax.nn.sigmoid(ra + ba_ref[...])
    i_gate = jax.nn.sigmoid(xa + bx_ref[...])
    lam = lam_ref[...]
    softplus_neg_lam = jnp.maximum(-lam, 0.0) + jnp.log1p(jnp.exp(-jnp.abs(lam)))
    log_a = -LRU_C * r_gate * softplus_neg_lam
    a = jnp.exp(log_a)
    one_minus_a2 = -jnp.tanh(log_a) * (a * a + 1.0)
    a_s[...] = a
    b_s[...] = jnp.sqrt(one_minus_a2) * (i_gate * u)

    @pl.when(s == 0)
    def _():
        h_s[...] = jnp.zeros_like(h_s)

    def body(t, h):
        tt = (tb - 1 - t) if reverse else t
        h = a_s[pl.ds(tt, 1), :] * h + b_s[pl.ds(tt, 1), :]
        hs_s[pl.ds(tt, 1), :] = h
        return h

    h_s[...] = lax.fori_loop(0, tb, body, h_s[...], unroll=8)

    if final:
        o_ref[...] = (jax.nn.gelu(g_ref[...]) * (hf_ref[...] + hs_s[...])).astype(o_ref.dtype)


def rglru_direction(proj, r_col, g_col, conv_w, conv_b, wa, ba, wx, bx, lam, h_fwd, *, batch, seq_len, ctx_len):
    rows = proj.shape[0]
    w = wa.shape[0] * wa.shape[1]
    tb = SEQ_BLOCK
    assert ctx_len == tb and seq_len % tb == 0
    n_lat = seq_len // tb
    ctx0 = batch * n_lat
    hb = tb // SUBLANES
    n_halo = rows // SUBLANES
    final = h_fwd is not None

    def blk(b, s):
        k = (n_lat - s) if final else (s - 1)
        return jnp.where(s == 0, ctx0 + b, b * n_lat + k)

    row_spec = lambda col: pl.BlockSpec((tb, w), lambda b, s: (blk(b, s), col))
    vec = lambda n: pl.BlockSpec((n, w), lambda b, s: (0, 0))
    mat = pl.BlockSpec(wa.shape, lambda b, s: (0, 0, 0))
    in_specs = [
        row_spec(r_col),
        pl.BlockSpec((SUBLANES, w), lambda b, s: (jnp.maximum(blk(b, s) * hb - 1, 0), r_col)),
        pl.BlockSpec((SUBLANES, w), lambda b, s: (jnp.minimum((blk(b, s) + 1) * hb, n_halo - 1), r_col)),
        vec(CONV_W), vec(1), mat, vec(1), mat, vec(1), vec(1),
    ]
    args = [proj, proj, proj, conv_w, conv_b.reshape(1, w), wa, ba.reshape(1, w), wx, bx.reshape(1, w),
            lam.reshape(1, w)]
    scratch = [pltpu.VMEM((tb, w), F32), pltpu.VMEM((tb, w), F32), pltpu.VMEM((1, w), F32)]
    if final:
        in_specs += [row_spec(0), row_spec(g_col)]
        args += [h_fwd, proj]
        scratch.append(pltpu.VMEM((tb, w), F32))
    return pl.pallas_call(
        functools.partial(_lru_kernel, reverse=final, n_lat=n_lat, final=final),
        grid=(batch, n_lat + 1),
        in_specs=in_specs,
        out_specs=row_spec(0),
        out_shape=jax.ShapeDtypeStruct((rows, w), BF16 if final else F32),
        scratch_shapes=scratch,
        compiler_params=_params("arbitrary", "arbitrary"),
        name="rglru_reverse" if final else "rglru_forward",
    )(*args)


def _rotate_halves(x, cos, sin):
    half = x.shape[-1] // 2
    x1, x2 = x[:, :half], x[:, half:]
    return jnp.concatenate([x1 * cos - x2 * sin, x1 * sin + x2 * cos], axis=-1)


def _retention_kernel(lg_ref, qf_ref, kf_ref, vf_ref, cf_ref, sf_ref, qb_ref, kb_ref, vb_ref, cb_ref, sb_ref,
                      of_ref, ob_ref, state, *, n_ctx):
    s = pl.program_id(1)
    c, width = qf_ref.shape
    dk = width // RET_HEADS
    is_lat = s >= n_ctx

    @pl.when(s == 0)
    def _():
        state[...] = jnp.zeros_like(state)

    ii = lax.broadcasted_iota(jnp.int32, (c, 1), 0).astype(F32)
    diff = ii - lax.broadcasted_iota(jnp.int32, (1, c), 1).astype(F32)
    k_scale = dk ** -0.5
    dirs = ((qf_ref, kf_ref, vf_ref, cf_ref, sf_ref, of_ref), (qb_ref, kb_ref, vb_ref, cb_ref, sb_ref, ob_ref))
    for d, (q_ref, k_ref, v_ref, cos_ref, sin_ref, o_ref) in enumerate(dirs):
        cos = jnp.where(is_lat, cos_ref[...], 1.0)
        sin = jnp.where(is_lat, sin_ref[...], 0.0)
        for h in range(RET_HEADS):
            lg = lg_ref[d, h]
            sl = slice(h * dk, (h + 1) * dk)
            q = _rotate_halves(q_ref[:, sl], cos, sin)
            k = _rotate_halves(k_ref[:, sl], cos, sin) * k_scale
            vb = v_ref[:, sl].astype(BF16)
            if d == 0:
                intra = jnp.where(diff >= 0, jnp.exp(lg * jnp.maximum(diff, 0.0)), 0.0)
                q_dec = jnp.exp(lg * (ii + 1.0))
                k_dec = jnp.exp(lg * (c - 1.0 - ii))
            else:
                intra = jnp.where(diff <= 0, jnp.exp(lg * jnp.maximum(-diff, 0.0)), 0.0)
                q_dec = jnp.exp(lg * (c - ii))
                k_dec = jnp.exp(lg * ii)
            s_dec = jnp.exp(lg * jnp.full((1, 1), float(c), F32))
            qb = q.astype(BF16)
            scores = lax.dot_general(qb, k.astype(BF16), (((1,), (1,)), ((), ())), preferred_element_type=F32)
            scores = scores * intra
            st = state[d, h]
            o = _dot(scores.astype(BF16), vb) + _dot((q * q_dec).astype(BF16), st.astype(BF16))
            o_ref[:, sl] = o
            kd_t = (k * k_dec).T.astype(BF16)
            state[d, h] = st * s_dec + _dot(kd_t, vb)


def retention_bidir(proj, q_col, k_col, v_col, log_g, cos, sin, *, batch, seq_len, ctx_len):
    rows = proj.shape[0]
    c = RET_CHUNK
    width = RET_HEADS * 2 * cos.shape[1]
    dk = width // RET_HEADS
    n_lat, n_ctx = seq_len // c, ctx_len // c
    ctx0 = batch * n_lat
    last = n_lat + n_ctx - 1

    def fwd(b, s):
        return jnp.where(s < n_ctx, ctx0 + b * n_ctx + s, b * n_lat + s - n_ctx)

    def bwd(b, s):
        return jnp.where(s < n_ctx, ctx0 + b * n_ctx + (n_ctx - 1 - s), b * n_lat + (last - s))

    def specs(idx):
        tab = pl.BlockSpec((c, dk // 2), lambda b, s: (jnp.where(s >= n_ctx, idx(b, s) - b * n_lat, 0), 0))
        return [pl.BlockSpec((c, width), lambda b, s, col=col: (idx(b, s), col)) for col in (q_col, k_col, v_col)] + [tab, tab]

    out_spec = lambda idx: pl.BlockSpec((c, width), lambda b, s: (idx(b, s), 0))
    return pl.pallas_call(
        functools.partial(_retention_kernel, n_ctx=n_ctx),
        grid=(batch, n_lat + n_ctx),
        in_specs=[pl.BlockSpec(memory_space=pltpu.SMEM)] + specs(fwd) + specs(bwd),
        out_specs=[out_spec(fwd), out_spec(bwd)],
        out_shape=[jax.ShapeDtypeStruct((rows, width), F32)] * 2,
        scratch_shapes=[pltpu.VMEM((2, RET_HEADS, dk, dk), F32)],
        compiler_params=_params("arbitrary", "arbitrary"),
        name="retention_bidir",
    )(log_g, proj, proj, proj, cos, sin, proj, proj, proj, cos, sin)


def _even_out_kernel(x_ref, mod_ref, gpost_ref, lru_ref, of_ref, ob_ref, gate_ref, gn_ref, w_ref, o_ref):
    o = of_ref[...] + ob_ref[...]
    width = o.shape[1]
    dv = width // RET_HEADS
    parts = []
    for h in range(RET_HEADS):
        oh = o[:, h * dv:(h + 1) * dv]
        mu = jnp.mean(oh, axis=-1, keepdims=True)
        cen = oh - mu
        var = jnp.mean(cen * cen, axis=-1, keepdims=True)
        parts.append(cen * lax.rsqrt(var + EPS))
    gate = gate_ref[...]
    ret = jnp.concatenate(parts, axis=-1) * gn_ref[...] * (gate * jax.nn.sigmoid(gate))
    lw = lru_ref.shape[1]
    y = _dot(lru_ref[...], w_ref[0:lw, :]) + _dot(ret.astype(BF16), w_ref[lw:lw + width, :])
    o_ref[...] = _gated_residual(x_ref[...], y, gpost_ref[...], mod_ref, 1.0)


def even_mixer_out(x, mod, g_post, lru, o_fwd, o_bwd, proj, gate_col, gn_g, w_out, seq_len):
    rows, d = x.shape
    tm = OUT_ROW_TILE
    lw, rw = lru.shape[1], o_fwd.shape[1]
    return pl.pallas_call(
        _even_out_kernel,
        grid=(rows // tm,),
        in_specs=[
            pl.BlockSpec((tm, d), lambda i: (i, 0)),
            _mod_spec(d, tm, seq_len, mod.shape[0]),
            pl.BlockSpec((1, d), lambda i: (0, 0)),
            pl.BlockSpec((tm, lw), lambda i: (i, 0)),
            pl.BlockSpec((tm, rw), lambda i: (i, 0)),
            pl.BlockSpec((tm, rw), lambda i: (i, 0)),
            pl.BlockSpec((tm, rw), lambda i: (i, gate_col)),
            pl.BlockSpec((1, rw), lambda i: (0, 0)),
            pl.BlockSpec((lw + rw, d), lambda i: (0, 0)),
        ],
        out_specs=pl.BlockSpec((tm, d), lambda i: (i, 0)),
        out_shape=jax.ShapeDtypeStruct((rows, d), F32),
        compiler_params=_params("arbitrary"),
        name="even_mixer_out",
    )(x, mod, g_post.reshape(1, d), lru, o_fwd, o_bwd, proj, gn_g.reshape(1, rw), w_out)


def _head_norm_rotary(x, g, cos_full, sin_signed):
    ms = jnp.mean(x * x, axis=-1, keepdims=True)
    y = x * lax.rsqrt(ms + EPS) * g
    return y * cos_full + pltpu.roll(y, HEAD_DIM // 2, axis=1) * sin_signed


def _kv_prep_kernel(k_ref, v_ref, kg_ref, cos_ref, sin_ref, ko_ref, vo_ref):
    is_lat = pl.program_id(1) > 0
    cos = jnp.where(is_lat, cos_ref[...], 1.0)
    sin = jnp.where(is_lat, sin_ref[...], 0.0)
    for h in range(KV_HEADS):
        sl = slice(h * HEAD_DIM, (h + 1) * HEAD_DIM)
        ko_ref[:, sl] = _head_norm_rotary(k_ref[:, sl], kg_ref[...], cos, sin).astype(BF16)
    vo_ref[...] = v_ref[...].astype(BF16)


def kv_prepare(proj, k_col, v_col, k_g, cos_full, sin_signed, *, batch, seq_len, ctx_len):
    tb = SEQ_BLOCK
    kvw = KV_HEADS * HEAD_DIM
    assert ctx_len == tb
    n_lat = seq_len // tb
    ctx0 = batch * n_lat
    src = lambda col: pl.BlockSpec((tb, kvw), lambda b, s: (jnp.where(s == 0, ctx0 + b, b * n_lat + s - 1), col))
    tab = pl.BlockSpec((tb, HEAD_DIM), lambda b, s: (jnp.maximum(s - 1, 0), 0))
    dst = pl.BlockSpec((tb, kvw), lambda b, s: (b * (n_lat + 1) + s, 0))
    out_rows = batch * (seq_len + ctx_len)
    return pl.pallas_call(
        _kv_prep_kernel,
        grid=(batch, n_lat + 1),
        in_specs=[src(k_col), src(v_col), pl.BlockSpec((1, HEAD_DIM), lambda b, s: (0, 0)), tab, tab],
        out_specs=[dst, dst],
        out_shape=[jax.ShapeDtypeStruct((out_rows, kvw), BF16)] * 2,
        compiler_params=_params("arbitrary", "arbitrary"),
        name="kv_prepare",
    )(proj, proj, k_g.reshape(1, HEAD_DIM), cos_full, sin_signed)


def _attention_kernel(q0_ref, q1_ref, q2_ref, qg_ref, cos_ref, sin_ref, k_ref, v_ref, o_ref):
    tq = q0_ref.shape[0]
    n_keys = k_ref.shape[0]
    tk = ATT_KV_TILE
    cos, sin = cos_ref[...], sin_ref[...]
    q = jnp.concatenate([_head_norm_rotary(r[...], qg_ref[...], cos, sin) for r in (q0_ref, q1_ref, q2_ref)],
                        axis=0).astype(BF16)
    scale = HEAD_DIM ** -0.5

    def body(j, carry):
        m, l, acc = carry
        start = pl.multiple_of(j * tk, 128)
        kt = k_ref[pl.ds(start, tk), :]
        vt = v_ref[pl.ds(start, tk), :]
        sc = lax.dot_general(q, kt, (((1,), (1,)), ((), ())), preferred_element_type=F32) * scale
        m_new = jnp.maximum(m, jnp.max(sc, axis=-1, keepdims=True))
        alpha = jnp.exp(m - m_new)
        p = jnp.exp(sc - m_new)
        l = alpha * l + jnp.sum(p, axis=-1, keepdims=True)
        acc = alpha * acc + _dot(p.astype(BF16), vt)
        return m_new, l, acc

    rows = GROUP * tq
    init = (jnp.full((rows, 1), -jnp.inf, F32), jnp.zeros((rows, 1), F32), jnp.zeros((rows, HEAD_DIM), F32))
    m, l, acc = lax.fori_loop(0, n_keys // tk, body, init)
    out = acc / l
    for g in range(GROUP):
        o_ref[:, g * HEAD_DIM:(g + 1) * HEAD_DIM] = out[g * tq:(g + 1) * tq].astype(o_ref.dtype)


def gqa_attention(proj, q_col0, q_g, cos_full, sin_signed, keys, values, *, batch, seq_len, ctx_len):
    tq = ATT_Q_TILE
    n_keys = seq_len + ctx_len
    assert seq_len % tq == 0 and n_keys % ATT_KV_TILE == 0
    nq = seq_len // tq
    qspec = lambda g: pl.BlockSpec((tq, HEAD_DIM), lambda b, kh, i: (b * nq + i, q_col0 + kh * GROUP + g))
    tab = pl.BlockSpec((tq, HEAD_DIM), lambda b, kh, i: (i, 0))
    kv = pl.BlockSpec((n_keys, HEAD_DIM), lambda b, kh, i: (b, kh))
    return pl.pallas_call(
        _attention_kernel,
        grid=(batch, KV_HEADS, nq),
        in_specs=[qspec(0), qspec(1), qspec(2), pl.BlockSpec((1, HEAD_DIM), lambda b, kh, i: (0, 0)), tab, tab, kv, kv],
        out_specs=pl.BlockSpec((tq, GROUP * HEAD_DIM), lambda b, kh, i: (b * nq + i, kh)),
        out_shape=jax.ShapeDtypeStruct((batch * seq_len, ATT_HEADS * HEAD_DIM), BF16),
        compiler_params=_params("arbitrary", "arbitrary", "arbitrary"),
        name="gqa_attention",
    )(proj, proj, proj, q_g.reshape(1, HEAD_DIM), cos_full, sin_signed, keys, values)


def _pool_kernel(x_ref, xp_ref, xn_ref, w_ref, scale_ref, o_ref, *, seq_len):
    i = pl.program_id(1)
    tb = x_ref.shape[0]
    x = x_ref[...]
    prev = jnp.where(i > 0, xp_ref[...], 0.0)
    nxt = jnp.where(i < pl.num_programs(1) - 1, xn_ref[...], 0.0)
    ext = jnp.concatenate([prev, x, nxt], axis=0)
    t = i * tb + lax.broadcasted_iota(jnp.int32, (tb, 1), 0)
    for gi, win in enumerate(POOL_WINDOWS):
        sl = slice(gi * POOL_GROUP, (gi + 1) * POOL_GROUP)
        half = win // 2
        e = ext[:, sl]
        tot = e[SUBLANES - half:SUBLANES - half + tb]
        for off in range(1 - half, half):
            tot = tot + e[SUBLANES + off:SUBLANES + off + tb]
        cnt = (jnp.minimum(t + half, seq_len) - jnp.maximum(t - half, 0)).astype(F32)
        centred = tot / cnt - x[:, sl]
        o_ref[:, sl] = (_dot(centred.astype(BF16), w_ref[gi]) * scale_ref[:, sl]).astype(o_ref.dtype)


def multiscale_pool(proj, pool_w, pool_scale, *, batch, seq_len):
    tb = POOL_BLOCK
    pw = POOL_GROUP * len(POOL_WINDOWS)
    nb = seq_len // tb
    hb = tb // SUBLANES
    assert max(POOL_WINDOWS) // 2 <= SUBLANES
    return pl.pallas_call(
        functools.partial(_pool_kernel, seq_len=seq_len),
        grid=(batch, nb),
        in_specs=[
            pl.BlockSpec((tb, pw), lambda b, i: (b * nb + i, 0)),
            pl.BlockSpec((SUBLANES, pw), lambda b, i: (jnp.maximum((b * nb + i) * hb - 1, 0), 0)),
            pl.BlockSpec((SUBLANES, pw), lambda b, i: ((b * nb + i + 1) * hb, 0)),
            pl.BlockSpec(pool_w.shape, lambda b, i: (0, 0, 0)),
            pl.BlockSpec((1, pw), lambda b, i: (0, 0)),
        ],
        out_specs=pl.BlockSpec((tb, pw), lambda b, i: (b * nb + i, 0)),
        out_shape=jax.ShapeDtypeStruct((batch * seq_len, pw), BF16),
        compiler_params=_params("arbitrary", "arbitrary"),
        name="multiscale_pool",
    )(proj, proj, proj, pool_w, pool_scale.reshape(1, pw))


def _odd_out_kernel(x_ref, mod_ref, gpost_ref, pool_ref, att_ref, w_ref, o_ref):
    pw = pool_ref.shape[1]
    y = _dot(pool_ref[...], w_ref[0:pw, :]) + _dot(att_ref[...], w_ref[pw:, :])
    o_ref[...] = _gated_residual(x_ref[...], y, gpost_ref[...], mod_ref, 1.0)


def odd_mixer_out(x, n_rows, mod, g_post, pooled, att, w_out, seq_len):
    d = x.shape[1]
    tm = OUT_ROW_TILE
    pw, aw = pooled.shape[1], att.shape[1]
    return pl.pallas_call(
        _odd_out_kernel,
        grid=(n_rows // tm,),
        in_specs=[
            pl.BlockSpec((tm, d), lambda i: (i, 0)),
            _mod_spec(d, tm, seq_len, mod.shape[0]),
            pl.BlockSpec((1, d), lambda i: (0, 0)),
            pl.BlockSpec((tm, pw), lambda i: (i, 0)),
            pl.BlockSpec((tm, aw), lambda i: (i, 0)),
            pl.BlockSpec((pw + aw, d), lambda i: (0, 0)),
        ],
        out_specs=pl.BlockSpec((tm, d), lambda i: (i, 0)),
        out_shape=jax.ShapeDtypeStruct((n_rows, d), F32),
        compiler_params=_params("arbitrary"),
        name="odd_mixer_out",
    )(x, mod, g_post.reshape(1, d), pooled, att, w_out)


def kernel(x, c, ctx, c_ctx, mod_w, mod_b, norm_pre, norm_post, ffn_gate, ffn_up, ffn_down,
           ev_w_in, ev_w_out, lru_conv_w, lru_conv_b, lru_wa, lru_ba, lru_wx, lru_bx, lru_lambda,
           ret_decay_logit, ret_gn, od_w_in, od_w_out, pool_w, pool_scale, q_norm, k_norm):
    B, S, D = x.shape
    Lc = ctx.shape[1]
    depth = mod_w.shape[0]
    assert depth == 2 and B < SUBLANES
    geom = dict(batch=B, seq_len=S, ctx_len=Lc)
    n_lat_rows = B * S

    grid_rows = S // GRID_W
    row = jnp.repeat(jnp.arange(grid_rows, dtype=F32), GRID_W)
    col = jnp.tile(jnp.arange(GRID_W, dtype=F32), grid_rows)
    n_ax = HEAD_DIM // 4
    f_ax = ROPE_THETA ** (-jnp.arange(n_ax, dtype=F32) / n_ax)
    ang2 = jnp.concatenate([row[:, None] * f_ax, col[:, None] * f_ax], axis=-1)
    cos2, sin2 = jnp.cos(ang2), jnp.sin(ang2)
    cos_full = jnp.concatenate([cos2, cos2], axis=-1)
    sin_signed = jnp.concatenate([-sin2, sin2], axis=-1)
    ret_dk = ev_w_out.shape[1] // 2 // RET_HEADS
    n_r = ret_dk // 2
    f_r = RET_THETA ** (-jnp.arange(n_r, dtype=F32) / n_r)
    ang1 = jnp.arange(S, dtype=F32)[:, None] * f_r
    cos1, sin1 = jnp.cos(ang1), jnp.sin(ang1)

    wg, wu, wd = ffn_gate.astype(BF16), ffn_up.astype(BF16), ffn_down.astype(BF16)

    c_all = jnp.zeros((SUBLANES, D), F32).at[:B].set(c).at[B].set(c_ctx)
    mods = modulation_vectors(c_all, mod_w, mod_b).reshape(depth, SUBLANES, 3, 3, D)[:, :B + 1]

    xs = jnp.concatenate([x.reshape(n_lat_rows, D), ctx.reshape(B * Lc, D)], axis=0)
    n_rows = xs.shape[0]

    li, e = 0, 0
    mod = lambda sub: mods[li, :, sub]
    xs = ffn_sublayer(xs, n_rows, mod(0), norm_pre[li, 0], norm_post[li, 0], wg[li, 0], wu[li, 0], wd[li, 0], S)
    proj = mixer_in_proj(xs, mod(1), norm_pre[li, 1], ev_w_in[e].astype(BF16), S)
    lru_args = lambda d: (lru_conv_w[e], lru_conv_b[e], lru_wa[e, d].astype(BF16), lru_ba[e, d],
                          lru_wx[e, d].astype(BF16), lru_bx[e, d], lru_lambda[e, d])
    h_fwd = rglru_direction(proj, 1, 0, *lru_args(0), None, **geom)
    lru = rglru_direction(proj, 1, 0, *lru_args(1), h_fwd, **geom)
    log_g = -jax.nn.softplus(-ret_decay_logit[e].astype(F32))
    o_fwd, o_bwd = retention_bidir(proj, 2, 3, 4, log_g, cos1, sin1, **geom)
    xs = even_mixer_out(xs, mod(1), norm_post[li, 1], lru, o_fwd, o_bwd, proj, 5, ret_gn[e],
                        ev_w_out[e].astype(BF16), S)
    xs = ffn_sublayer(xs, n_rows, mod(2), norm_pre[li, 2], norm_post[li, 2], wg[li, 1], wu[li, 1], wd[li, 1], S)

    li, o = 1, 0
    xs = ffn_sublayer(xs, n_rows, mod(0), norm_pre[li, 0], norm_post[li, 0], wg[li, 0], wu[li, 0], wd[li, 0], S)
    proj = mixer_in_proj(xs, mod(1), norm_pre[li, 1], od_w_in[o].astype(BF16), S)
    pw = POOL_GROUP * len(POOL_WINDOWS)
    kvw = KV_HEADS * HEAD_DIM
    q0 = pw // HEAD_DIM
    k_col = (pw + ATT_HEADS * HEAD_DIM) // kvw
    keys, values = kv_prepare(proj, k_col, k_col + 1, k_norm[o], cos_full, sin_signed, **geom)
    att = gqa_attention(proj, q0, q_norm[o], cos_full, sin_signed, keys, values, **geom)
    pooled = multiscale_pool(proj, pool_w[o].astype(BF16), pool_scale[o], batch=B, seq_len=S)
    xl = odd_mixer_out(xs, n_lat_rows, mod(1), norm_post[li, 1], pooled, att, od_w_out[o].astype(BF16), S)
    xl = ffn_sublayer(xl, n_lat_rows, mod(2), norm_pre[li, 2], norm_post[li, 2], wg[li, 1], wu[li, 1], wd[li, 1], S)
    return xl.reshape(B, S, D)
```

```python
import functools

import jax
import jax.numpy as jnp
from jax import lax
from jax.experimental import pallas as pl
from jax.experimental.pallas import tpu as pltpu

GRID_W = 64
EPS = 1e-6
FFN_STEP = 0.5
LRU_BLOCKS = 8
LRU_C = 8.0
CONV_W = 4
CONV_LEFT = 2
RET_HEADS = 4
RET_CHUNK = 128
RET_THETA = 10000.0
POOL_WINDOWS = (2, 4, 8, 16)
POOL_GROUP = 128
ATT_HEADS = 12
KV_HEADS = 4
GROUP = ATT_HEADS // KV_HEADS
HEAD_DIM = 128
ROPE_THETA = 10000.0

VMEM_LIMIT_BYTES = 56 * 1024 * 1024
SUBLANES = 8

ROW_TILE = 512
OUT_ROW_TILE = 256
FFN_FF_TILE = 512
PROJ_COL_TILE = 1024
SEQ_BLOCK = 256
ATT_Q_TILE = 512
ATT_KV_TILE = 768
POOL_BLOCK = 512
LOG2_E = 1.4426950408889634

BF16 = jnp.bfloat16
F32 = jnp.float32


def _params(*sem):
    return pltpu.CompilerParams(dimension_semantics=sem, vmem_limit_bytes=VMEM_LIMIT_BYTES)


def _dot(a, b):
    return jnp.dot(a, b, preferred_element_type=F32)


def _modulated(x, g_pre, mod_ref):
    ms = jnp.mean(x * x, axis=-1, keepdims=True)
    y = x * lax.rsqrt(ms + EPS) * g_pre
    return y * (1.0 + mod_ref[0, 1:2, :]) + mod_ref[0, 0:1, :]


def _gated_residual(x, y, g_post, mod_ref, step):
    ms = jnp.mean(y * y, axis=-1, keepdims=True)
    yn = y * lax.rsqrt(ms + EPS) * g_post
    return x + (step * mod_ref[0, 2:3, :]) * yn


def _mod_spec(d, tm, seq_len, n_groups):
    return pl.BlockSpec((1, 3, d), lambda i, *_: (jnp.minimum((i * tm) // seq_len, n_groups - 1), 0, 0))


def _mod_kernel(c_ref, w_ref, b_ref, o_ref):
    c = c_ref[...]
    sc = c * jax.nn.sigmoid(c)
    o_ref[0] = _dot(sc.astype(BF16), w_ref[0].astype(BF16)) + b_ref[0]


def modulation_vectors(c_all, mod_w, mod_b):
    depth, d, n = mod_w.shape
    tn = 1024
    return pl.pallas_call(
        _mod_kernel,
        grid=(depth, n // tn),
        in_specs=[
            pl.BlockSpec((SUBLANES, d), lambda l, j: (0, 0)),
            pl.BlockSpec((1, d, tn), lambda l, j: (l, 0, j)),
            pl.BlockSpec((1, 1, tn), lambda l, j: (l, 0, j)),
        ],
        out_specs=pl.BlockSpec((1, SUBLANES, tn), lambda l, j: (l, 0, j)),
        out_shape=jax.ShapeDtypeStruct((depth, SUBLANES, n), F32),
        compiler_params=_params("arbitrary", "arbitrary"),
        name="modulation_vectors",
    )(c_all, mod_w, mod_b.reshape(depth, 1, n))


def _ffn_kernel(x_ref, mod_ref, gpre_ref, gpost_ref, wg_ref, wu_ref, wd_ref, o_ref, h_ref, acc_ref):
    j = pl.program_id(1)

    @pl.when(j == 0)
    def _():
        h_ref[...] = _modulated(x_ref[...], gpre_ref[...], mod_ref).astype(BF16)
        acc_ref[...] = jnp.zeros_like(acc_ref)

    h = h_ref[...]
    g = _dot(h, wg_ref[...])
    u = _dot(h, wu_ref[...])
    a = (g * jax.nn.sigmoid(g)) * u
    acc_ref[...] += _dot(a.astype(BF16), wd_ref[...])

    @pl.when(j == pl.num_programs(1) - 1)
    def _():
        o_ref[...] = _gated_residual(x_ref[...], acc_ref[...], gpost_ref[...], mod_ref, FFN_STEP)


def ffn_sublayer(x, n_rows, mod, g_pre, g_post, w_gate, w_up, w_down, seq_len):
    d = x.shape[1]
    f = w_gate.shape[1]
    tm, tf = ROW_TILE, FFN_FF_TILE
    assert n_rows % tm == 0 and f % tf == 0 and seq_len % tm == 0
    return pl.pallas_call(
        _ffn_kernel,
        grid=(n_rows // tm, f // tf),
        in_specs=[
            pl.BlockSpec((tm, d), lambda i, j: (i, 0)),
            _mod_spec(d, tm, seq_len, mod.shape[0]),
            pl.BlockSpec((1, d), lambda i, j: (0, 0)),
            pl.BlockSpec((1, d), lambda i, j: (0, 0)),
            pl.BlockSpec((d, tf), lambda i, j: (0, j)),
            pl.BlockSpec((d, tf), lambda i, j: (0, j)),
            pl.BlockSpec((tf, d), lambda i, j: (j, 0)),
        ],
        out_specs=pl.BlockSpec((tm, d), lambda i, j: (i, 0)),
        out_shape=jax.ShapeDtypeStruct((n_rows, d), F32),
        scratch_shapes=[pltpu.VMEM((tm, d), BF16), pltpu.VMEM((tm, d), F32)],
        compiler_params=_params("arbitrary", "arbitrary"),
        name="ffn_sublayer",
    )(x, mod, g_pre.reshape(1, d), g_post.reshape(1, d), w_gate, w_up, w_down)


def _in_proj_kernel(x_ref, mod_ref, gpre_ref, w_ref, o_ref, h_ref):
    @pl.when(pl.program_id(1) == 0)
    def _():
        h_ref[...] = _modulated(x_ref[...], gpre_ref[...], mod_ref).astype(BF16)

    o_ref[...] = _dot(h_ref[...], w_ref[...])


def mixer_in_proj(x, mod, g_pre, w_in, seq_len):
    rows, d = x.shape
    n = w_in.shape[1]
    tm, tn = ROW_TILE, PROJ_COL_TILE
    assert rows % tm == 0 and n % tn == 0
    return pl.pallas_call(
        _in_proj_kernel,
        grid=(rows // tm, n // tn),
        in_specs=[
            pl.BlockSpec((tm, d), lambda i, j: (i, 0)),
            _mod_spec(d, tm, seq_len, mod.shape[0]),
            pl.BlockSpec((1, d), lambda i, j: (0, 0)),
            pl.BlockSpec((d, tn), lambda i, j: (0, j)),
        ],
        out_specs=pl.BlockSpec((tm, tn), lambda i, j: (i, j)),
        out_shape=jax.ShapeDtypeStruct((rows, n), F32),
        scratch_shapes=[pltpu.VMEM((tm, d), BF16)],
        compiler_params=_params("arbitrary", "arbitrary"),
        name="mixer_in_proj",
    )(x, mod, g_pre.reshape(1, d), w_in)


def _lru_kernel(*refs, reverse, n_lat, final):
    if final:
        (r_ref, rp_ref, rn_ref, cw_ref, cb_ref, wa_ref, ba_ref, wx_ref, bx_ref, lam_ref,
         hf_ref, g_ref, o_ref, a_s, b_s, h_s, hs_s) = refs
    else:
        (r_ref, rp_ref, rn_ref, cw_ref, cb_ref, wa_ref, ba_ref, wx_ref, bx_ref, lam_ref,
         o_ref, a_s, b_s, h_s) = refs
        hs_s = o_ref
    s = pl.program_id(1)
    tb, w = r_ref.shape
    is_lat = s > 0
    k = (n_lat - s) if reverse else (s - 1)
    has_prev = jnp.logical_and(is_lat, k > 0)
    has_next = jnp.logical_and(is_lat, k < n_lat - 1)

    prev = jnp.where(has_prev, rp_ref[...], 0.0)
    nxt = jnp.where(has_next, rn_ref[...], 0.0)
    ext = jnp.concatenate([prev, r_ref[...], nxt], axis=0)
    base = SUBLANES - CONV_LEFT
    u = ext[base:base + tb] * cw_ref[0:1, :]
    for t in range(1, CONV_W):
        u = u + ext[base + t:base + t + tb] * cw_ref[t:t + 1, :]
    u = u + cb_ref[...]

    ub = u.astype(BF16)
    blk = w // LRU_BLOCKS
    ra = jnp.concatenate([_dot(ub[:, n * blk:(n + 1) * blk], wa_ref[n]) for n in range(LRU_BLOCKS)], axis=1)
    xa = jnp.concatenate([_dot(ub[:, n * blk:(n + 1) * blk], wx_ref[n]) for n in range(LRU_BLOCKS)], axis=1)
    r_gate = jax.nn.sigmoid(ra + ba_ref[...])
    i_gate = jax.nn.sigmoid(xa + bx_ref[...])
    lam = lam_ref[...]
    softplus_neg_lam = jnp.maximum(-lam, 0.0) + jnp.log1p(jnp.exp(-jnp.abs(lam)))
    log_a = -LRU_C * r_gate * softplus_neg_lam
    a = jnp.exp(log_a)
    one_minus_a2 = -jnp.tanh(log_a) * (a * a + 1.0)
    a_s[...] = a
    b_s[...] = jnp.sqrt(one_minus_a2) * (i_gate * u)

    @pl.when(s == 0)
    def _():
        h_s[...] = jnp.zeros_like(h_s)

    def body(t, h):
        tt = (tb - 1 - t) if reverse else t
        h = a_s[pl.ds(tt, 1), :] * h + b_s[pl.ds(tt, 1), :]
        hs_s[pl.ds(tt, 1), :] = h
        return h

    h_s[...] = lax.fori_loop(0, tb, body, h_s[...], unroll=8)

    if final:
        o_ref[...] = (jax.nn.gelu(g_ref[...]) * (hf_ref[...] + hs_s[...])).astype(o_ref.dtype)


def rglru_direction(proj, r_col, g_col, conv_w, conv_b, wa, ba, wx, bx, lam, h_fwd, *, batch, seq_len, ctx_len):
    rows = proj.shape[0]
    w = wa.shape[0] * wa.shape[1]
    tb = SEQ_BLOCK
    assert ctx_len == tb and seq_len % tb == 0
    n_lat = seq_len // tb
    ctx0 = batch * n_lat
    hb = tb // SUBLANES
    n_halo = rows // SUBLANES
    final = h_fwd is not None

    def blk(b, s):
        k = (n_lat - s) if final else (s - 1)
        return jnp.where(s == 0, ctx0 + b, b * n_lat + k)

    row_spec = lambda col: pl.BlockSpec((tb, w), lambda b, s: (blk(b, s), col))
    vec = lambda n: pl.BlockSpec((n, w), lambda b, s: (0, 0))
    mat = pl.BlockSpec(wa.shape, lambda b, s: (0, 0, 0))
    in_specs = [
        row_spec(r_col),
        pl.BlockSpec((SUBLANES, w), lambda b, s: (jnp.maximum(blk(b, s) * hb - 1, 0), r_col)),
        pl.BlockSpec((SUBLANES, w), lambda b, s: (jnp.minimum((blk(b, s) + 1) * hb, n_halo - 1), r_col)),
        vec(CONV_W), vec(1), mat, vec(1), mat, vec(1), vec(1),
    ]
    args = [proj, proj, proj, conv_w, conv_b.reshape(1, w), wa, ba.reshape(1, w), wx, bx.reshape(1, w),
            lam.reshape(1, w)]
    scratch = [pltpu.VMEM((tb, w), F32), pltpu.VMEM((tb, w), F32), pltpu.VMEM((1, w), F32)]
    if final:
        in_specs += [row_spec(0), row_spec(g_col)]
        args += [h_fwd, proj]
        scratch.append(pltpu.VMEM((tb, w), F32))
    return pl.pallas_call(
        functools.partial(_lru_kernel, reverse=final, n_lat=n_lat, final=final),
        grid=(batch, n_lat + 1),
        in_specs=in_specs,
        out_specs=row_spec(0),
        out_shape=jax.ShapeDtypeStruct((rows, w), BF16 if final else F32),
        scratch_shapes=scratch,
        compiler_params=_params("arbitrary", "arbitrary"),
        name="rglru_reverse" if final else "rglru_forward",
    )(*args)


def _rotate_halves(x, cos, sin):
    half = x.shape[-1] // 2
    x1, x2 = x[:, :half], x[:, half:]
    return jnp.concatenate([x1 * cos - x2 * sin, x1 * sin + x2 * cos], axis=-1)


def _retention_kernel(lg_ref, qf_ref, kf_ref, vf_ref, cf_ref, sf_ref, qb_ref, kb_ref, vb_ref, cb_ref, sb_ref,
                      of_ref, ob_ref, state, *, n_ctx):
    s = pl.program_id(1)
    c, width = qf_ref.shape
    dk = width // RET_HEADS
    is_lat = s >= n_ctx

    @pl.when(s == 0)
    def _():
        state[...] = jnp.zeros_like(state)

    ii = lax.broadcasted_iota(jnp.int32, (c, 1), 0).astype(F32)
    diff = ii - lax.broadcasted_iota(jnp.int32, (1, c), 1).astype(F32)
    k_scale = dk ** -0.5
    dirs = ((qf_ref, kf_ref, vf_ref, cf_ref, sf_ref, of_ref), (qb_ref, kb_ref, vb_ref, cb_ref, sb_ref, ob_ref))
    for d, (q_ref, k_ref, v_ref, cos_ref, sin_ref, o_ref) in enumerate(dirs):
        cos = jnp.where(is_lat, cos_ref[...], 1.0)
        sin = jnp.where(is_lat, sin_ref[...], 0.0)
        for h in range(RET_HEADS):
            lg = lg_ref[d, h]
            sl = slice(h * dk, (h + 1) * dk)
            q = _rotate_halves(q_ref[:, sl], cos, sin)
            k = _rotate_halves(k_ref[:, sl], cos, sin) * k_scale
            vb = v_ref[:, sl].astype(BF16)
            if d == 0:
                intra = jnp.where(diff >= 0, jnp.exp(lg * jnp.maximum(diff, 0.0)), 0.0)
                q_dec = jnp.exp(lg * (ii + 1.0))
                k_dec = jnp.exp(lg * (c - 1.0 - ii))
            else:
                intra = jnp.where(diff <= 0, jnp.exp(lg * jnp.maximum(-diff, 0.0)), 0.0)
                q_dec = jnp.exp(lg * (c - ii))
                k_dec = jnp.exp(lg * ii)
            s_dec = jnp.exp(lg * jnp.full((1, 1), float(c), F32))
            qb = q.astype(BF16)
            scores = lax.dot_general(qb, k.astype(BF16), (((1,), (1,)), ((), ())), preferred_element_type=F32)
            scores = scores * intra
            st = state[d, h]
            o = _dot(scores.astype(BF16), vb) + _dot((q * q_dec).astype(BF16), st.astype(BF16))
            o_ref[:, sl] = o
            kd_t = (k * k_dec).T.astype(BF16)
            state[d, h] = st * s_dec + _dot(kd_t, vb)


def retention_bidir(proj, q_col, k_col, v_col, log_g, cos, sin, *, batch, seq_len, ctx_len):
    rows = proj.shape[0]
    c = RET_CHUNK
    width = RET_HEADS * 2 * cos.shape[1]
    dk = width // RET_HEADS
    n_lat, n_ctx = seq_len // c, ctx_len // c
    ctx0 = batch * n_lat
    last = n_lat + n_ctx - 1

    def fwd(b, s):
        return jnp.where(s < n_ctx, ctx0 + b * n_ctx + s, b * n_lat + s - n_ctx)

    def bwd(b, s):
        return jnp.where(s < n_ctx, ctx0 + b * n_ctx + (n_ctx - 1 - s), b * n_lat + (last - s))

    def specs(idx):
        tab = pl.BlockSpec((c, dk // 2), lambda b, s: (jnp.where(s >= n_ctx, idx(b, s) - b * n_lat, 0), 0))
        return [pl.BlockSpec((c, width), lambda b, s, col=col: (idx(b, s), col)) for col in (q_col, k_col, v_col)] + [tab, tab]

    out_spec = lambda idx: pl.BlockSpec((c, width), lambda b, s: (idx(b, s), 0))
    return pl.pallas_call(
        functools.partial(_retention_kernel, n_ctx=n_ctx),
        grid=(batch, n_lat + n_ctx),
        in_specs=[pl.BlockSpec(memory_space=pltpu.SMEM)] + specs(fwd) + specs(bwd),
        out_specs=[out_spec(fwd), out_spec(bwd)],
        out_shape=[jax.ShapeDtypeStruct((rows, width), F32)] * 2,
        scratch_shapes=[pltpu.VMEM((2, RET_HEADS, dk, dk), F32)],
        compiler_params=_params("arbitrary", "arbitrary"),
        name="retention_bidir",
    )(log_g, proj, proj, proj, cos, sin, proj, proj, proj, cos, sin)


def _even_out_kernel(x_ref, mod_ref, gpost_ref, lru_ref, of_ref, ob_ref, gate_ref, gn_ref, w_ref, o_ref):
    o = of_ref[...] + ob_ref[...]
    width = o.shape[1]
    dv = width // RET_HEADS
    parts = []
    for h in range(RET_HEADS):
        oh = o[:, h * dv:(h + 1) * dv]
        mu = jnp.mean(oh, axis=-1, keepdims=True)
        cen = oh - mu
        var = jnp.mean(cen * cen, axis=-1, keepdims=True)
        parts.append(cen * lax.rsqrt(var + EPS))
    gate = gate_ref[...]
    ret = jnp.concatenate(parts, axis=-1) * gn_ref[...] * (gate * jax.nn.sigmoid(gate))
    lw = lru_ref.shape[1]
    y = _dot(lru_ref[...], w_ref[0:lw, :]) + _dot(ret.astype(BF16), w_ref[lw:lw + width, :])
    o_ref[...] = _gated_residual(x_ref[...], y, gpost_ref[...], mod_ref, 1.0)


def even_mixer_out(x, mod, g_post, lru, o_fwd, o_bwd, proj, gate_col, gn_g, w_out, seq_len):
    rows, d = x.shape
    tm = OUT_ROW_TILE
    lw, rw = lru.shape[1], o_fwd.shape[1]
    return pl.pallas_call(
        _even_out_kernel,
        grid=(rows // tm,),
        in_specs=[
            pl.BlockSpec((tm, d), lambda i: (i, 0)),
            _mod_spec(d, tm, seq_len, mod.shape[0]),
            pl.BlockSpec((1, d), lambda i: (0, 0)),
            pl.BlockSpec((tm, lw), lambda i: (i, 0)),
            pl.BlockSpec((tm, rw), lambda i: (i, 0)),
            pl.BlockSpec((tm, rw), lambda i: (i, 0)),
            pl.BlockSpec((tm, rw), lambda i: (i, gate_col)),
            pl.BlockSpec((1, rw), lambda i: (0, 0)),
            pl.BlockSpec((lw + rw, d), lambda i: (0, 0)),
        ],
        out_specs=pl.BlockSpec((tm, d), lambda i: (i, 0)),
        out_shape=jax.ShapeDtypeStruct((rows, d), F32),
        compiler_params=_params("arbitrary"),
        name="even_mixer_out",
    )(x, mod, g_post.reshape(1, d), lru, o_fwd, o_bwd, proj, gn_g.reshape(1, rw), w_out)


def _head_norm_rotary(x, g, cos_full, sin_signed):
    ms = jnp.mean(x * x, axis=-1, keepdims=True)
    y = x * lax.rsqrt(ms + EPS) * g
    return y * cos_full + pltpu.roll(y, HEAD_DIM // 2, axis=1) * sin_signed


def _kv_prep_kernel(k_ref, v_ref, kg_ref, cos_ref, sin_ref, ko_ref, vo_ref):
    is_lat = pl.program_id(1) > 0
    cos = jnp.where(is_lat, cos_ref[...], 1.0)
    sin = jnp.where(is_lat, sin_ref[...], 0.0)
    for h in range(KV_HEADS):
        sl = slice(h * HEAD_DIM, (h + 1) * HEAD_DIM)
        ko_ref[:, sl] = _head_norm_rotary(k_ref[:, sl], kg_ref[...], cos, sin).astype(BF16)
        vo_ref[sl, :] = v_ref[:, sl].T.astype(BF16)


def kv_prepare(proj, k_col, v_col, k_g, cos_full, sin_signed, *, batch, seq_len, ctx_len):
    tb = SEQ_BLOCK
    kvw = KV_HEADS * HEAD_DIM
    assert ctx_len == tb
    n_lat = seq_len // tb
    ctx0 = batch * n_lat
    n_keys = seq_len + ctx_len
    src = lambda col: pl.BlockSpec((tb, kvw), lambda b, s: (jnp.where(s == 0, ctx0 + b, b * n_lat + s - 1), col))
    tab = pl.BlockSpec((tb, HEAD_DIM), lambda b, s: (jnp.maximum(s - 1, 0), 0))
    return pl.pallas_call(
        _kv_prep_kernel,
        grid=(batch, n_lat + 1),
        in_specs=[src(k_col), src(v_col), pl.BlockSpec((1, HEAD_DIM), lambda b, s: (0, 0)), tab, tab],
        out_specs=[pl.BlockSpec((tb, kvw), lambda b, s: (b * (n_lat + 1) + s, 0)),
                   pl.BlockSpec((kvw, tb), lambda b, s: (b, s))],
        out_shape=[jax.ShapeDtypeStruct((batch * n_keys, kvw), BF16),
                   jax.ShapeDtypeStruct((batch * kvw, n_keys), BF16)],
        compiler_params=_params("arbitrary", "arbitrary"),
        name="kv_prepare",
    )(proj, proj, k_g.reshape(1, HEAD_DIM), cos_full, sin_signed)


def _attention_kernel(q0_ref, q1_ref, q2_ref, qg_ref, cos_ref, sin_ref, k_ref, vt_ref, o_ref, s_ref):
    tq = q0_ref.shape[0]
    n_keys = k_ref.shape[0]
    tk = ATT_KV_TILE
    cos, sin = cos_ref[...], sin_ref[...]
    q_t = [_head_norm_rotary(r[...], qg_ref[...], cos, sin).T.astype(BF16) for r in (q0_ref, q1_ref, q2_ref)]
    c = HEAD_DIM ** -0.5 * LOG2_E

    n_chunks = n_keys // tk

    def scores(j, g):
        start = pl.multiple_of(j * tk, tk)
        return _dot(k_ref[pl.ds(start, tk), :], q_t[g])

    def update(j, s, state):
        m, l, acc = state
        start = pl.multiple_of(j * tk, tk)
        m_new = jnp.maximum(m, jnp.max(s, axis=0, keepdims=True))
        alpha = jnp.exp2((m - m_new) * c)
        p = jnp.exp2((s - m_new) * c)
        l = alpha * l + jnp.sum(p, axis=0, keepdims=True)
        acc = alpha * acc + _dot(vt_ref[:, pl.ds(start, tk)], p.astype(BF16))
        return m_new, l, acc

    s_ref[...] = scores(0, 0)

    def body(j, carry):
        state = list(carry)
        s_cur = s_ref[...]
        for g in range(GROUP):
            s_next = scores(j, g + 1) if g + 1 < GROUP else scores(jnp.minimum(j + 1, n_chunks - 1), 0)
            state[g] = update(j, s_cur, state[g])
            s_cur = s_next
        s_ref[...] = s_cur
        return tuple(state)

    init = tuple((jnp.full((1, tq), -jnp.inf, F32), jnp.zeros((1, tq), F32), jnp.zeros((HEAD_DIM, tq), F32))
                 for _ in range(GROUP))
    final = lax.fori_loop(0, n_chunks, body, init, unroll=True)
    for g, (_, l, acc) in enumerate(final):
        o_ref[:, g * HEAD_DIM:(g + 1) * HEAD_DIM] = (acc / l).T.astype(o_ref.dtype)


def gqa_attention(proj, q_col0, q_g, cos_full, sin_signed, keys, values_t, *, batch, seq_len, ctx_len):
    tq = ATT_Q_TILE
    n_keys = seq_len + ctx_len
    assert seq_len % tq == 0 and n_keys % ATT_KV_TILE == 0
    nq = seq_len // tq
    qspec = lambda g: pl.BlockSpec((tq, HEAD_DIM), lambda b, kh, i: (b * nq + i, q_col0 + kh * GROUP + g))
    tab = pl.BlockSpec((tq, HEAD_DIM), lambda b, kh, i: (i, 0))
    return pl.pallas_call(
        _attention_kernel,
        grid=(batch, KV_HEADS, nq),
        in_specs=[qspec(0), qspec(1), qspec(2), pl.BlockSpec((1, HEAD_DIM), lambda b, kh, i: (0, 0)), tab, tab,
                  pl.BlockSpec((n_keys, HEAD_DIM), lambda b, kh, i: (b, kh)),
                  pl.BlockSpec((HEAD_DIM, n_keys), lambda b, kh, i: (b * KV_HEADS + kh, 0))],
        out_specs=pl.BlockSpec((tq, GROUP * HEAD_DIM), lambda b, kh, i: (b * nq + i, kh)),
        out_shape=jax.ShapeDtypeStruct((batch * seq_len, ATT_HEADS * HEAD_DIM), BF16),
        scratch_shapes=[pltpu.VMEM((ATT_KV_TILE, tq), F32)],
        compiler_params=_params("arbitrary", "arbitrary", "arbitrary"),
        name="gqa_attention",
    )(proj, proj, proj, q_g.reshape(1, HEAD_DIM), cos_full, sin_signed, keys, values_t)


def _pool_kernel(x_ref, xp_ref, xn_ref, w_ref, scale_ref, o_ref, *, seq_len):
    i = pl.program_id(1)
    tb = x_ref.shape[0]
    x = x_ref[...]
    prev = jnp.where(i > 0, xp_ref[...], 0.0)
    nxt = jnp.where(i < pl.num_programs(1) - 1, xn_ref[...], 0.0)
    ext = jnp.concatenate([prev, x, nxt], axis=0)
    t = i * tb + lax.broadcasted_iota(jnp.int32, (tb, 1), 0)
    for gi, win in enumerate(POOL_WINDOWS):
        sl = slice(gi * POOL_GROUP, (gi + 1) * POOL_GROUP)
        half = win // 2
        e = ext[:, sl]
        tot = e[SUBLANES - half:SUBLANES - half + tb]
        for off in range(1 - half, half):
            tot = tot + e[SUBLANES + off:SUBLANES + off + tb]
        cnt = (jnp.minimum(t + half, seq_len) - jnp.maximum(t - half, 0)).astype(F32)
        centred = tot / cnt - x[:, sl]
        o_ref[:, sl] = (_dot(centred.astype(BF16), w_ref[gi]) * scale_ref[:, sl]).astype(o_ref.dtype)


def multiscale_pool(proj, pool_w, pool_scale, *, batch, seq_len):
    tb = POOL_BLOCK
    pw = POOL_GROUP * len(POOL_WINDOWS)
    nb = seq_len // tb
    hb = tb // SUBLANES
    assert max(POOL_WINDOWS) // 2 <= SUBLANES
    return pl.pallas_call(
        functools.partial(_pool_kernel, seq_len=seq_len),
        grid=(batch, nb),
        in_specs=[
            pl.BlockSpec((tb, pw), lambda b, i: (b * nb + i, 0)),
            pl.BlockSpec((SUBLANES, pw), lambda b, i: (jnp.maximum((b * nb + i) * hb - 1, 0), 0)),
            pl.BlockSpec((SUBLANES, pw), lambda b, i: ((b * nb + i + 1) * hb, 0)),
            pl.BlockSpec(pool_w.shape, lambda b, i: (0, 0, 0)),
            pl.BlockSpec((1, pw), lambda b, i: (0, 0)),
        ],
        out_specs=pl.BlockSpec((tb, pw), lambda b, i: (b * nb + i, 0)),
        out_shape=jax.ShapeDtypeStruct((batch * seq_len, pw), BF16),
        compiler_params=_params("arbitrary", "arbitrary"),
        name="multiscale_pool",
    )(proj, proj, proj, pool_w, pool_scale.reshape(1, pw))


def _odd_out_kernel(x_ref, mod_ref, gpost_ref, pool_ref, att_ref, w_ref, o_ref):
    pw = pool_ref.shape[1]
    y = _dot(pool_ref[...], w_ref[0:pw, :]) + _dot(att_ref[...], w_ref[pw:, :])
    o_ref[...] = _gated_residual(x_ref[...], y, gpost_ref[...], mod_ref, 1.0)


def odd_mixer_out(x, n_rows, mod, g_post, pooled, att, w_out, seq_len):
    d = x.shape[1]
    tm = OUT_ROW_TILE
    pw, aw = pooled.shape[1], att.shape[1]
    return pl.pallas_call(
        _odd_out_kernel,
        grid=(n_rows // tm,),
        in_specs=[
            pl.BlockSpec((tm, d), lambda i: (i, 0)),
            _mod_spec(d, tm, seq_len, mod.shape[0]),
            pl.BlockSpec((1, d), lambda i: (0, 0)),
            pl.BlockSpec((tm, pw), lambda i: (i, 0)),
            pl.BlockSpec((tm, aw), lambda i: (i, 0)),
            pl.BlockSpec((pw + aw, d), lambda i: (0, 0)),
        ],
        out_specs=pl.BlockSpec((tm, d), lambda i: (i, 0)),
        out_shape=jax.ShapeDtypeStruct((n_rows, d), F32),
        compiler_params=_params("arbitrary"),
        name="odd_mixer_out",
    )(x, mod, g_post.reshape(1, d), pooled, att, w_out)


def kernel(x, c, ctx, c_ctx, mod_w, mod_b, norm_pre, norm_post, ffn_gate, ffn_up, ffn_down,
           ev_w_in, ev_w_out, lru_conv_w, lru_conv_b, lru_wa, lru_ba, lru_wx, lru_bx, lru_lambda,
           ret_decay_logit, ret_gn, od_w_in, od_w_out, pool_w, pool_scale, q_norm, k_norm):
    B, S, D = x.shape
    Lc = ctx.shape[1]
    depth = mod_w.shape[0]
    assert depth == 2 and B < SUBLANES
    geom = dict(batch=B, seq_len=S, ctx_len=Lc)
    n_lat_rows = B * S

    grid_rows = S // GRID_W
    row = jnp.repeat(jnp.arange(grid_rows, dtype=F32), GRID_W)
    col = jnp.tile(jnp.arange(GRID_W, dtype=F32), grid_rows)
    n_ax = HEAD_DIM // 4
    f_ax = ROPE_THETA ** (-jnp.arange(n_ax, dtype=F32) / n_ax)
    ang2 = jnp.concatenate([row[:, None] * f_ax, col[:, None] * f_ax], axis=-1)
    cos2, sin2 = jnp.cos(ang2), jnp.sin(ang2)
    cos_full = jnp.concatenate([cos2, cos2], axis=-1)
    sin_signed = jnp.concatenate([-sin2, sin2], axis=-1)
    ret_dk = ev_w_out.shape[1] // 2 // RET_HEADS
    n_r = ret_dk // 2
    f_r = RET_THETA ** (-jnp.arange(n_r, dtype=F32) / n_r)
    ang1 = jnp.arange(S, dtype=F32)[:, None] * f_r
    cos1, sin1 = jnp.cos(ang1), jnp.sin(ang1)

    wg, wu, wd = ffn_gate.astype(BF16), ffn_up.astype(BF16), ffn_down.astype(BF16)

    c_all = jnp.zeros((SUBLANES, D), F32).at[:B].set(c).at[B].set(c_ctx)
    mods = modulation_vectors(c_all, mod_w, mod_b).reshape(depth, SUBLANES, 3, 3, D)[:, :B + 1]

    xs = jnp.concatenate([x.reshape(n_lat_rows, D), ctx.reshape(B * Lc, D)], axis=0)
    n_rows = xs.shape[0]

    li, e = 0, 0
    mod = lambda sub: mods[li, :, sub]
    xs = ffn_sublayer(xs, n_rows, mod(0), norm_pre[li, 0], norm_post[li, 0], wg[li, 0], wu[li, 0], wd[li, 0], S)
    proj = mixer_in_proj(xs, mod(1), norm_pre[li, 1], ev_w_in[e].astype(BF16), S)
    lru_args = lambda d: (lru_conv_w[e], lru_conv_b[e], lru_wa[e, d].astype(BF16), lru_ba[e, d],
                          lru_wx[e, d].astype(BF16), lru_bx[e, d], lru_lambda[e, d])
    h_fwd = rglru_direction(proj, 1, 0, *lru_args(0), None, **geom)
    lru = rglru_direction(proj, 1, 0, *lru_args(1), h_fwd, **geom)
    log_g = -jax.nn.softplus(-ret_decay_logit[e].astype(F32))
    o_fwd, o_bwd = retention_bidir(proj, 2, 3, 4, log_g, cos1, sin1, **geom)
    xs = even_mixer_out(xs, mod(1), norm_post[li, 1], lru, o_fwd, o_bwd, proj, 5, ret_gn[e],
                        ev_w_out[e].astype(BF16), S)
    xs = ffn_sublayer(xs, n_rows, mod(2), norm_pre[li, 2], norm_post[li, 2], wg[li, 1], wu[li, 1], wd[li, 1], S)

    li, o = 1, 0
    xs = ffn_sublayer(xs, n_rows, mod(0), norm_pre[li, 0], norm_post[li, 0], wg[li, 0], wu[li, 0], wd[li, 0], S)
    proj = mixer_in_proj(xs, mod(1), norm_pre[li, 1], od_w_in[o].astype(BF16), S)
    pw = POOL_GROUP * len(POOL_WINDOWS)
    kvw = KV_HEADS * HEAD_DIM
    q0 = pw // HEAD_DIM
    k_col = (pw + ATT_HEADS * HEAD_DIM) // kvw
    keys, values_t = kv_prepare(proj, k_col, k_col + 1, k_norm[o], cos_full, sin_signed, **geom)
    att = gqa_attention(proj, q0, q_norm[o], cos_full, sin_signed, keys, values_t, **geom)
    pooled = multiscale_pool(proj, pool_w[o].astype(BF16), pool_scale[o], batch=B, seq_len=S)
    xl = odd_mixer_out(xs, n_lat_rows, mod(1), norm_post[li, 1], pooled, att, od_w_out[o].astype(BF16), S)
    xl = ffn_sublayer(xl, n_lat_rows, mod(2), norm_pre[li, 2], norm_post[li, 2], wg[li, 1], wu[li, 1], wd[li, 1], S)
    return xl.reshape(B, S, D)
```

```python
import functools

import jax
import jax.numpy as jnp
from jax import lax
from jax.experimental import pallas as pl
from jax.experimental.pallas import tpu as pltpu

GRID_W = 64
EPS = 1e-6
FFN_STEP = 0.5
LRU_BLOCKS = 8
LRU_C = 8.0
CONV_W = 4
CONV_LEFT = 2
RET_HEADS = 4
RET_CHUNK = 128
RET_THETA = 10000.0
POOL_WINDOWS = (2, 4, 8, 16)
POOL_GROUP = 128
ATT_HEADS = 12
KV_HEADS = 4
GROUP = ATT_HEADS // KV_HEADS
HEAD_DIM = 128
ROPE_THETA = 10000.0

VMEM_LIMIT_BYTES = 56 * 1024 * 1024
SUBLANES = 8

ROW_TILE = 512
PROJ_ROW_TILE = 1024
OUT_ROW_TILE = 256
FFN_FF_TILE = 512
PROJ_COL_TILE = 1024
SEQ_BLOCK = 256
ATT_Q_TILE = 512
ATT_KV_TILE = 1408
POOL_BLOCK = 512
LOG2_E = 1.4426950408889634
NORM_ROWS = 16
NORM_UNROLL = 16

BF16 = jnp.bfloat16
F32 = jnp.float32


def _params(*sem):
    return pltpu.CompilerParams(dimension_semantics=sem, vmem_limit_bytes=VMEM_LIMIT_BYTES)


def _dot(a, b):
    return jnp.dot(a, b, preferred_element_type=F32)


def _modulate_rows(x_ref, gpre_ref, mod_ref, row_s, h_ref):
    d = x_ref.shape[1]
    tiles = NORM_ROWS // SUBLANES
    row_s[0] = jnp.broadcast_to(gpre_ref[...] * (1.0 + mod_ref[0, 1:2, :]), (SUBLANES, d))
    row_s[1] = jnp.broadcast_to(mod_ref[0, 0:1, :], (SUBLANES, d))

    def body(i, carry):
        r = pl.multiple_of(i * NORM_ROWS, NORM_ROWS)
        x = x_ref[pl.ds(r, NORM_ROWS), :]
        ms = jnp.mean(x * x, axis=-1, keepdims=True)
        xn = (x * lax.rsqrt(ms + EPS)).reshape(tiles, SUBLANES, d)
        h = xn * row_s[0] + row_s[1]
        h_ref[pl.ds(r, NORM_ROWS), :] = h.reshape(NORM_ROWS, d).astype(h_ref.dtype)
        return carry

    lax.fori_loop(0, x_ref.shape[0] // NORM_ROWS, body, 0, unroll=NORM_UNROLL)


def _gated_residual_rows(x_ref, y_ref, gpost_ref, mod_ref, row_s, o_ref, step):
    d = x_ref.shape[1]
    tiles = NORM_ROWS // SUBLANES
    row_s[2] = jnp.broadcast_to((step * mod_ref[0, 2:3, :]) * gpost_ref[...], (SUBLANES, d))

    def body(i, carry):
        r = pl.multiple_of(i * NORM_ROWS, NORM_ROWS)
        y = y_ref[pl.ds(r, NORM_ROWS), :]
        ms = jnp.mean(y * y, axis=-1, keepdims=True)
        yn = (y * lax.rsqrt(ms + EPS)).reshape(tiles, SUBLANES, d)
        o_ref[pl.ds(r, NORM_ROWS), :] = x_ref[pl.ds(r, NORM_ROWS), :] + (yn * row_s[2]).reshape(NORM_ROWS, d)
        return carry

    lax.fori_loop(0, x_ref.shape[0] // NORM_ROWS, body, 0, unroll=NORM_UNROLL)


def _mod_spec(d, tm, seq_len, n_groups):
    return pl.BlockSpec((1, 3, d), lambda i, *_: (jnp.minimum((i * tm) // seq_len, n_groups - 1), 0, 0))


def _mod_kernel(c_ref, w_ref, b_ref, o_ref):
    c = c_ref[...]
    sc = c * jax.nn.sigmoid(c)
    o_ref[0] = _dot(sc.astype(BF16), w_ref[0].astype(BF16)) + b_ref[0]


def modulation_vectors(c_all, mod_w, mod_b):
    depth, d, n = mod_w.shape
    tn = 1024
    return pl.pallas_call(
        _mod_kernel,
        grid=(depth, n // tn),
        in_specs=[
            pl.BlockSpec((SUBLANES, d), lambda l, j: (0, 0)),
            pl.BlockSpec((1, d, tn), lambda l, j: (l, 0, j)),
            pl.BlockSpec((1, 1, tn), lambda l, j: (l, 0, j)),
        ],
        out_specs=pl.BlockSpec((1, SUBLANES, tn), lambda l, j: (l, 0, j)),
        out_shape=jax.ShapeDtypeStruct((depth, SUBLANES, n), F32),
        compiler_params=_params("arbitrary", "arbitrary"),
        name="modulation_vectors",
    )(c_all, mod_w, mod_b.reshape(depth, 1, n))


def _ffn_kernel(x_ref, mod_ref, gpre_ref, gpost_ref, wg_ref, wu_ref, wd_ref, o_ref, h_ref, acc_ref, row_s):
    j = pl.program_id(1)

    @pl.when(j == 0)
    def _():
        _modulate_rows(x_ref, gpre_ref, mod_ref, row_s, h_ref)
        acc_ref[...] = jnp.zeros_like(acc_ref)

    h = h_ref[...]
    g = _dot(h, wg_ref[...])
    u = _dot(h, wu_ref[...])
    a = (g * jax.nn.sigmoid(g)) * u
    acc_ref[...] += _dot(a.astype(BF16), wd_ref[...])

    @pl.when(j == pl.num_programs(1) - 1)
    def _():
        _gated_residual_rows(x_ref, acc_ref, gpost_ref, mod_ref, row_s, o_ref, FFN_STEP)


def ffn_sublayer(x, n_rows, mod, g_pre, g_post, w_gate, w_up, w_down, which, seq_len):
    d = x.shape[1]
    f = w_gate.shape[-1]
    tm, tf = ROW_TILE, FFN_FF_TILE
    li, fi = which
    assert n_rows % tm == 0 and f % tf == 0 and seq_len % tm == 0
    return pl.pallas_call(
        _ffn_kernel,
        grid=(n_rows // tm, f // tf),
        in_specs=[
            pl.BlockSpec((tm, d), lambda i, j: (i, 0)),
            _mod_spec(d, tm, seq_len, mod.shape[0]),
            pl.BlockSpec((1, d), lambda i, j: (0, 0)),
            pl.BlockSpec((1, d), lambda i, j: (0, 0)),
            pl.BlockSpec((None, None, d, tf), lambda i, j: (li, fi, 0, j)),
            pl.BlockSpec((None, None, d, tf), lambda i, j: (li, fi, 0, j)),
            pl.BlockSpec((None, None, tf, d), lambda i, j: (li, fi, j, 0)),
        ],
        out_specs=pl.BlockSpec((tm, d), lambda i, j: (i, 0)),
        out_shape=jax.ShapeDtypeStruct((n_rows, d), F32),
        scratch_shapes=[pltpu.VMEM((tm, d), BF16), pltpu.VMEM((tm, d), F32), pltpu.VMEM((3, SUBLANES, d), F32)],
        compiler_params=_params("arbitrary", "arbitrary"),
        name="ffn_sublayer",
    )(x, mod, g_pre.reshape(1, d), g_post.reshape(1, d), w_gate, w_up, w_down)


def _in_proj_kernel(x_ref, mod_ref, gpre_ref, w_ref, o_ref, h_ref, row_s):
    @pl.when(pl.program_id(1) == 0)
    def _():
        _modulate_rows(x_ref, gpre_ref, mod_ref, row_s, h_ref)

    o_ref[...] = _dot(h_ref[...], w_ref[...])


def mixer_in_proj(x, mod, g_pre, w_in, seq_len):
    rows, d = x.shape
    n = w_in.shape[1]
    tm, tn = PROJ_ROW_TILE, PROJ_COL_TILE
    assert rows % tm == 0 and n % tn == 0 and seq_len % tm == 0
    return pl.pallas_call(
        _in_proj_kernel,
        grid=(rows // tm, n // tn),
        in_specs=[
            pl.BlockSpec((tm, d), lambda i, j: (i, 0)),
            _mod_spec(d, tm, seq_len, mod.shape[0]),
            pl.BlockSpec((1, d), lambda i, j: (0, 0)),
            pl.BlockSpec((d, tn), lambda i, j: (0, j)),
        ],
        out_specs=pl.BlockSpec((tm, tn), lambda i, j: (i, j)),
        out_shape=jax.ShapeDtypeStruct((rows, n), F32),
        scratch_shapes=[pltpu.VMEM((tm, d), BF16), pltpu.VMEM((3, SUBLANES, d), F32)],
        compiler_params=_params("arbitrary", "arbitrary"),
        name="mixer_in_proj",
    )(x, mod, g_pre.reshape(1, d), w_in)


def _lru_kernel(*refs, reverse, n_lat, final):
    if final:
        (r_ref, rp_ref, rn_ref, cw_ref, cb_ref, wa_ref, ba_ref, wx_ref, bx_ref, lam_ref,
         hf_ref, g_ref, o_ref, a_s, b_s, h_s, hs_s) = refs
    else:
        (r_ref, rp_ref, rn_ref, cw_ref, cb_ref, wa_ref, ba_ref, wx_ref, bx_ref, lam_ref,
         o_ref, a_s, b_s, h_s) = refs
        hs_s = o_ref
    s = pl.program_id(1)
    tb, w = r_ref.shape
    is_lat = s > 0
    k = (n_lat - s) if reverse else (s - 1)
    has_prev = jnp.logical_and(is_lat, k > 0)
    has_next = jnp.logical_and(is_lat, k < n_lat - 1)

    prev = jnp.where(has_prev, rp_ref[...], 0.0)
    nxt = jnp.where(has_next, rn_ref[...], 0.0)
    ext = jnp.concatenate([prev, r_ref[...], nxt], axis=0)
    base = SUBLANES - CONV_LEFT
    u = ext[base:base + tb] * cw_ref[0:1, :]
    for t in range(1, CONV_W):
        u = u + ext[base + t:base + t + tb] * cw_ref[t:t + 1, :]
    u = u + cb_ref[...]

    ub = u.astype(BF16)
    blk = w // LRU_BLOCKS
    ra = jnp.concatenate([_dot(ub[:, n * blk:(n + 1) * blk], wa_ref[n]) for n in range(LRU_BLOCKS)], axis=1)
    xa = jnp.concatenate([_dot(ub[:, n * blk:(n + 1) * blk], wx_ref[n]) for n in range(LRU_BLOCKS)], axis=1)
    r_gate = jax.nn.sigmoid(ra + ba_ref[...])
    i_gate = jax.nn.sigmoid(xa + bx_ref[...])
    lam = lam_ref[...]
    softplus_neg_lam = jnp.maximum(-lam, 0.0) + jnp.log1p(jnp.exp(-jnp.abs(lam)))
    log_a = -LRU_C * r_gate * softplus_neg_lam
    a = jnp.exp(log_a)
    one_minus_a2 = -jnp.tanh(log_a) * (a * a + 1.0)
    a_s[...] = a
    b_s[...] = jnp.sqrt(one_minus_a2) * (i_gate * u)

    @pl.when(s == 0)
    def _():
        h_s[...] = jnp.zeros_like(h_s)

    def body(t, h):
        tt = (tb - 1 - t) if reverse else t
        h = a_s[pl.ds(tt, 1), :] * h + b_s[pl.ds(tt, 1), :]
        hs_s[pl.ds(tt, 1), :] = h
        return h

    h_s[...] = lax.fori_loop(0, tb, body, h_s[...], unroll=8)

    if final:
        o_ref[...] = (jax.nn.gelu(g_ref[...]) * (hf_ref[...] + hs_s[...])).astype(o_ref.dtype)


def rglru_direction(proj, r_col, g_col, conv_w, conv_b, wa, ba, wx, bx, lam, h_fwd, *, batch, seq_len, ctx_len):
    rows = proj.shape[0]
    w = wa.shape[0] * wa.shape[1]
    tb = SEQ_BLOCK
    assert ctx_len == tb and seq_len % tb == 0
    n_lat = seq_len // tb
    ctx0 = batch * n_lat
    hb = tb // SUBLANES
    n_halo = rows // SUBLANES
    final = h_fwd is not None

    def blk(b, s):
        k = (n_lat - s) if final else (s - 1)
        return jnp.where(s == 0, ctx0 + b, b * n_lat + k)

    row_spec = lambda col: pl.BlockSpec((tb, w), lambda b, s: (blk(b, s), col))
    vec = lambda n: pl.BlockSpec((n, w), lambda b, s: (0, 0))
    mat = pl.BlockSpec(wa.shape, lambda b, s: (0, 0, 0))
    in_specs = [
        row_spec(r_col),
        pl.BlockSpec((SUBLANES, w), lambda b, s: (jnp.maximum(blk(b, s) * hb - 1, 0), r_col)),
        pl.BlockSpec((SUBLANES, w), lambda b, s: (jnp.minimum((blk(b, s) + 1) * hb, n_halo - 1), r_col)),
        vec(CONV_W), vec(1), mat, vec(1), mat, vec(1), vec(1),
    ]
    args = [proj, proj, proj, conv_w, conv_b.reshape(1, w), wa, ba.reshape(1, w), wx, bx.reshape(1, w),
            lam.reshape(1, w)]
    scratch = [pltpu.VMEM((tb, w), F32), pltpu.VMEM((tb, w), F32), pltpu.VMEM((1, w), F32)]
    if final:
        in_specs += [row_spec(0), row_spec(g_col)]
        args += [h_fwd, proj]
        scratch.append(pltpu.VMEM((tb, w), F32))
    return pl.pallas_call(
        functools.partial(_lru_kernel, reverse=final, n_lat=n_lat, final=final),
        grid=(batch, n_lat + 1),
        in_specs=in_specs,
        out_specs=row_spec(0),
        out_shape=jax.ShapeDtypeStruct((rows, w), BF16 if final else F32),
        scratch_shapes=scratch,
        compiler_params=_params("arbitrary", "arbitrary"),
        name="rglru_reverse" if final else "rglru_forward",
    )(*args)


def _rotate_halves(x, cos, sin):
    half = x.shape[-1] // 2
    x1, x2 = x[:, :half], x[:, half:]
    return jnp.concatenate([x1 * cos - x2 * sin, x1 * sin + x2 * cos], axis=-1)


def _retention_kernel(lg_ref, qf_ref, kf_ref, vf_ref, cf_ref, sf_ref, qb_ref, kb_ref, vb_ref, cb_ref, sb_ref,
                      of_ref, ob_ref, state, *, n_ctx):
    s = pl.program_id(1)
    c, width = qf_ref.shape
    dk = width // RET_HEADS
    is_lat = s >= n_ctx

    @pl.when(s == 0)
    def _():
        state[...] = jnp.zeros_like(state)

    ii = lax.broadcasted_iota(jnp.int32, (c, 1), 0).astype(F32)
    diff = ii - lax.broadcasted_iota(jnp.int32, (1, c), 1).astype(F32)
    k_scale = dk ** -0.5
    dirs = ((qf_ref, kf_ref, vf_ref, cf_ref, sf_ref, of_ref), (qb_ref, kb_ref, vb_ref, cb_ref, sb_ref, ob_ref))
    for d, (q_ref, k_ref, v_ref, cos_ref, sin_ref, o_ref) in enumerate(dirs):
        cos = jnp.where(is_lat, cos_ref[...], 1.0)
        sin = jnp.where(is_lat, sin_ref[...], 0.0)
        for h in range(RET_HEADS):
            lg = lg_ref[d, h]
            sl = slice(h * dk, (h + 1) * dk)
            q = _rotate_halves(q_ref[:, sl], cos, sin)
            k = _rotate_halves(k_ref[:, sl], cos, sin) * k_scale
            vb = v_ref[:, sl].astype(BF16)
            if d == 0:
                intra = jnp.where(diff >= 0, jnp.exp(lg * jnp.maximum(diff, 0.0)), 0.0)
                q_dec = jnp.exp(lg * (ii + 1.0))
                k_dec = jnp.exp(lg * (c - 1.0 - ii))
            else:
                intra = jnp.where(diff <= 0, jnp.exp(lg * jnp.maximum(-diff, 0.0)), 0.0)
                q_dec = jnp.exp(lg * (c - ii))
                k_dec = jnp.exp(lg * ii)
            s_dec = jnp.exp(lg * jnp.full((1, 1), float(c), F32))
            qb = q.astype(BF16)
            scores = lax.dot_general(qb, k.astype(BF16), (((1,), (1,)), ((), ())), preferred_element_type=F32)
            scores = scores * intra
            st = state[d, h]
            o = _dot(scores.astype(BF16), vb) + _dot((q * q_dec).astype(BF16), st.astype(BF16))
            o_ref[:, sl] = o
            kd_t = (k * k_dec).T.astype(BF16)
            state[d, h] = st * s_dec + _dot(kd_t, vb)


def retention_bidir(proj, q_col, k_col, v_col, log_g, cos, sin, *, batch, seq_len, ctx_len):
    rows = proj.shape[0]
    c = RET_CHUNK
    width = RET_HEADS * 2 * cos.shape[1]
    dk = width // RET_HEADS
    n_lat, n_ctx = seq_len // c, ctx_len // c
    ctx0 = batch * n_lat
    last = n_lat + n_ctx - 1

    def fwd(b, s):
        return jnp.where(s < n_ctx, ctx0 + b * n_ctx + s, b * n_lat + s - n_ctx)

    def bwd(b, s):
        return jnp.where(s < n_ctx, ctx0 + b * n_ctx + (n_ctx - 1 - s), b * n_lat + (last - s))

    def specs(idx):
        tab = pl.BlockSpec((c, dk // 2), lambda b, s: (jnp.where(s >= n_ctx, idx(b, s) - b * n_lat, 0), 0))
        return [pl.BlockSpec((c, width), lambda b, s, col=col: (idx(b, s), col)) for col in (q_col, k_col, v_col)] + [tab, tab]

    out_spec = lambda idx: pl.BlockSpec((c, width), lambda b, s: (idx(b, s), 0))
    return pl.pallas_call(
        functools.partial(_retention_kernel, n_ctx=n_ctx),
        grid=(batch, n_lat + n_ctx),
        in_specs=[pl.BlockSpec(memory_space=pltpu.SMEM)] + specs(fwd) + specs(bwd),
        out_specs=[out_spec(fwd), out_spec(bwd)],
        out_shape=[jax.ShapeDtypeStruct((rows, width), F32)] * 2,
        scratch_shapes=[pltpu.VMEM((2, RET_HEADS, dk, dk), F32)],
        compiler_params=_params("arbitrary", "arbitrary"),
        name="retention_bidir",
    )(log_g, proj, proj, proj, cos, sin, proj, proj, proj, cos, sin)


def _even_out_kernel(x_ref, mod_ref, gpost_ref, lru_ref, of_ref, ob_ref, gate_ref, gn_ref, w_ref, o_ref, y_s, row_s):
    o = of_ref[...] + ob_ref[...]
    width = o.shape[1]
    dv = width // RET_HEADS
    parts = []
    for h in range(RET_HEADS):
        oh = o[:, h * dv:(h + 1) * dv]
        mu = jnp.mean(oh, axis=-1, keepdims=True)
        cen = oh - mu
        var = jnp.mean(cen * cen, axis=-1, keepdims=True)
        parts.append(cen * lax.rsqrt(var + EPS))
    gate = gate_ref[...]
    ret = jnp.concatenate(parts, axis=-1) * gn_ref[...] * (gate * jax.nn.sigmoid(gate))
    lw = lru_ref.shape[1]
    y_s[...] = _dot(lru_ref[...], w_ref[0:lw, :]) + _dot(ret.astype(BF16), w_ref[lw:lw + width, :])
    _gated_residual_rows(x_ref, y_s, gpost_ref, mod_ref, row_s, o_ref, 1.0)


def even_mixer_out(x, mod, g_post, lru, o_fwd, o_bwd, proj, gate_col, gn_g, w_out, seq_len):
    rows, d = x.shape
    tm = OUT_ROW_TILE
    lw, rw = lru.shape[1], o_fwd.shape[1]
    return pl.pallas_call(
        _even_out_kernel,
        grid=(rows // tm,),
        in_specs=[
            pl.BlockSpec((tm, d), lambda i: (i, 0)),
            _mod_spec(d, tm, seq_len, mod.shape[0]),
            pl.BlockSpec((1, d), lambda i: (0, 0)),
            pl.BlockSpec((tm, lw), lambda i: (i, 0)),
            pl.BlockSpec((tm, rw), lambda i: (i, 0)),
            pl.BlockSpec((tm, rw), lambda i: (i, 0)),
            pl.BlockSpec((tm, rw), lambda i: (i, gate_col)),
            pl.BlockSpec((1, rw), lambda i: (0, 0)),
            pl.BlockSpec((lw + rw, d), lambda i: (0, 0)),
        ],
        out_specs=pl.BlockSpec((tm, d), lambda i: (i, 0)),
        out_shape=jax.ShapeDtypeStruct((rows, d), F32),
        scratch_shapes=[pltpu.VMEM((tm, d), F32), pltpu.VMEM((3, SUBLANES, d), F32)],
        compiler_params=_params("arbitrary"),
        name="even_mixer_out",
    )(x, mod, g_post.reshape(1, d), lru, o_fwd, o_bwd, proj, gn_g.reshape(1, rw), w_out)


def _head_norm_rotary(x, g, cos_full, sin_signed):
    ms = jnp.mean(x * x, axis=-1, keepdims=True)
    y = x * lax.rsqrt(ms + EPS) * g
    return y * cos_full + pltpu.roll(y, HEAD_DIM // 2, axis=1) * sin_signed


def _kv_prep_kernel(k_ref, v_ref, kg_ref, cos_ref, sin_ref, ko_ref, vo_ref):
    is_lat = pl.program_id(1) > 0
    cos = jnp.where(is_lat, cos_ref[...], 1.0)
    sin = jnp.where(is_lat, sin_ref[...], 0.0)
    for h in range(KV_HEADS):
        sl = slice(h * HEAD_DIM, (h + 1) * HEAD_DIM)
        ko_ref[:, sl] = _head_norm_rotary(k_ref[:, sl], kg_ref[...], cos, sin).astype(BF16)
        vo_ref[sl, :] = v_ref[:, sl].T.astype(BF16)


def kv_prepare(proj, k_col, v_col, k_g, cos_full, sin_signed, *, batch, seq_len, ctx_len):
    tb = SEQ_BLOCK
    kvw = KV_HEADS * HEAD_DIM
    assert ctx_len == tb
    n_lat = seq_len // tb
    ctx0 = batch * n_lat
    n_keys = seq_len + ctx_len
    src = lambda col: pl.BlockSpec((tb, kvw), lambda b, s: (jnp.where(s == 0, ctx0 + b, b * n_lat + s - 1), col))
    tab = pl.BlockSpec((tb, HEAD_DIM), lambda b, s: (jnp.maximum(s - 1, 0), 0))
    return pl.pallas_call(
        _kv_prep_kernel,
        grid=(batch, n_lat + 1),
        in_specs=[src(k_col), src(v_col), pl.BlockSpec((1, HEAD_DIM), lambda b, s: (0, 0)), tab, tab],
        out_specs=[pl.BlockSpec((tb, kvw), lambda b, s: (b * (n_lat + 1) + s, 0)),
                   pl.BlockSpec((kvw, tb), lambda b, s: (b, s))],
        out_shape=[jax.ShapeDtypeStruct((batch * n_keys, kvw), BF16),
                   jax.ShapeDtypeStruct((batch * kvw, n_keys), BF16)],
        compiler_params=_params("arbitrary", "arbitrary"),
        name="kv_prepare",
    )(proj, proj, k_g.reshape(1, HEAD_DIM), cos_full, sin_signed)


def _attention_kernel(q0_ref, q1_ref, q2_ref, qg_ref, cos_ref, sin_ref, k_ref, vt_ref, o_ref, s_ref):
    tq = q0_ref.shape[0]
    n_keys = k_ref.shape[0]
    tk = ATT_KV_TILE
    cos, sin = cos_ref[...], sin_ref[...]
    c = HEAD_DIM ** -0.5 * LOG2_E
    q_t = [(_head_norm_rotary(r[...], qg_ref[...], cos, sin) * c).T.astype(BF16) for r in (q0_ref, q1_ref, q2_ref)]

    n_chunks = n_keys // tk

    def scores(j, g):
        start = pl.multiple_of(j * tk, tk)
        return _dot(k_ref[pl.ds(start, tk), :], q_t[g])

    def update(j, s, state):
        m, l, acc = state
        start = pl.multiple_of(j * tk, tk)
        m_new = jnp.maximum(m, jnp.max(s, axis=0, keepdims=True))
        alpha = jnp.exp2(m - m_new)
        p = jnp.exp2(s - m_new)
        l = alpha * l + jnp.sum(p, axis=0, keepdims=True)
        acc = alpha * acc + _dot(vt_ref[:, pl.ds(start, tk)], p.astype(BF16))
        return m_new, l, acc

    s_ref[...] = scores(0, 0)

    def body(j, carry):
        state = list(carry)
        s_cur = s_ref[...]
        for g in range(GROUP):
            s_next = scores(j, g + 1) if g + 1 < GROUP else scores(jnp.minimum(j + 1, n_chunks - 1), 0)
            state[g] = update(j, s_cur, state[g])
            s_cur = s_next
        s_ref[...] = s_cur
        return tuple(state)

    init = tuple((jnp.full((1, tq), -jnp.inf, F32), jnp.zeros((1, tq), F32), jnp.zeros((HEAD_DIM, tq), F32))
                 for _ in range(GROUP))
    final = lax.fori_loop(0, n_chunks, body, init, unroll=True)
    for g, (_, l, acc) in enumerate(final):
        o_ref[:, g * HEAD_DIM:(g + 1) * HEAD_DIM] = (acc / l).T.astype(o_ref.dtype)


def gqa_attention(proj, q_col0, q_g, cos_full, sin_signed, keys, values_t, *, batch, seq_len, ctx_len):
    tq = ATT_Q_TILE
    n_keys = seq_len + ctx_len
    assert seq_len % tq == 0 and n_keys % ATT_KV_TILE == 0
    nq = seq_len // tq
    qspec = lambda g: pl.BlockSpec((tq, HEAD_DIM), lambda b, kh, i: (b * nq + i, q_col0 + kh * GROUP + g))
    tab = pl.BlockSpec((tq, HEAD_DIM), lambda b, kh, i: (i, 0))
    return pl.pallas_call(
        _attention_kernel,
        grid=(batch, KV_HEADS, nq),
        in_specs=[qspec(0), qspec(1), qspec(2), pl.BlockSpec((1, HEAD_DIM), lambda b, kh, i: (0, 0)), tab, tab,
                  pl.BlockSpec((n_keys, HEAD_DIM), lambda b, kh, i: (b, kh)),
                  pl.BlockSpec((HEAD_DIM, n_keys), lambda b, kh, i: (b * KV_HEADS + kh, 0))],
        out_specs=pl.BlockSpec((tq, GROUP * HEAD_DIM), lambda b, kh, i: (b * nq + i, kh)),
        out_shape=jax.ShapeDtypeStruct((batch * seq_len, ATT_HEADS * HEAD_DIM), BF16),
        scratch_shapes=[pltpu.VMEM((ATT_KV_TILE, tq), F32)],
        compiler_params=_params("arbitrary", "arbitrary", "arbitrary"),
        name="gqa_attention",
    )(proj, proj, proj, q_g.reshape(1, HEAD_DIM), cos_full, sin_signed, keys, values_t)


def _pool_kernel(x_ref, xp_ref, xn_ref, w_ref, scale_ref, o_ref, *, seq_len):
    i = pl.program_id(1)
    tb = x_ref.shape[0]
    x = x_ref[...]
    prev = jnp.where(i > 0, xp_ref[...], 0.0)
    nxt = jnp.where(i < pl.num_programs(1) - 1, xn_ref[...], 0.0)
    ext = jnp.concatenate([prev, x, nxt], axis=0)
    t = i * tb + lax.broadcasted_iota(jnp.int32, (tb, 1), 0)
    for gi, win in enumerate(POOL_WINDOWS):
        sl = slice(gi * POOL_GROUP, (gi + 1) * POOL_GROUP)
        half = win // 2
        e = ext[:, sl]
        tot = e[SUBLANES - half:SUBLANES - half + tb]
        for off in range(1 - half, half):
            tot = tot + e[SUBLANES + off:SUBLANES + off + tb]
        cnt = (jnp.minimum(t + half, seq_len) - jnp.maximum(t - half, 0)).astype(F32)
        centred = tot / cnt - x[:, sl]
        o_ref[:, sl] = (_dot(centred.astype(BF16), w_ref[gi]) * scale_ref[:, sl]).astype(o_ref.dtype)


def multiscale_pool(proj, pool_w, pool_scale, *, batch, seq_len):
    tb = POOL_BLOCK
    pw = POOL_GROUP * len(POOL_WINDOWS)
    nb = seq_len // tb
    hb = tb // SUBLANES
    assert max(POOL_WINDOWS) // 2 <= SUBLANES
    return pl.pallas_call(
        functools.partial(_pool_kernel, seq_len=seq_len),
        grid=(batch, nb),
        in_specs=[
            pl.BlockSpec((tb, pw), lambda b, i: (b * nb + i, 0)),
            pl.BlockSpec((SUBLANES, pw), lambda b, i: (jnp.maximum((b * nb + i) * hb - 1, 0), 0)),
            pl.BlockSpec((SUBLANES, pw), lambda b, i: ((b * nb + i + 1) * hb, 0)),
            pl.BlockSpec(pool_w.shape, lambda b, i: (0, 0, 0)),
            pl.BlockSpec((1, pw), lambda b, i: (0, 0)),
        ],
        out_specs=pl.BlockSpec((tb, pw), lambda b, i: (b * nb + i, 0)),
        out_shape=jax.ShapeDtypeStruct((batch * seq_len, pw), BF16),
        compiler_params=_params("arbitrary", "arbitrary"),
        name="multiscale_pool",
    )(proj, proj, proj, pool_w, pool_scale.reshape(1, pw))


def _odd_out_kernel(x_ref, mod_ref, gpost_ref, pool_ref, att_ref, w_ref, o_ref, y_s, row_s):
    pw = pool_ref.shape[1]
    y_s[...] = _dot(pool_ref[...], w_ref[0:pw, :]) + _dot(att_ref[...], w_ref[pw:, :])
    _gated_residual_rows(x_ref, y_s, gpost_ref, mod_ref, row_s, o_ref, 1.0)


def odd_mixer_out(x, n_rows, mod, g_post, pooled, att, w_out, seq_len):
    d = x.shape[1]
    tm = OUT_ROW_TILE
    pw, aw = pooled.shape[1], att.shape[1]
    return pl.pallas_call(
        _odd_out_kernel,
        grid=(n_rows // tm,),
        in_specs=[
            pl.BlockSpec((tm, d), lambda i: (i, 0)),
            _mod_spec(d, tm, seq_len, mod.shape[0]),
            pl.BlockSpec((1, d), lambda i: (0, 0)),
            pl.BlockSpec((tm, pw), lambda i: (i, 0)),
            pl.BlockSpec((tm, aw), lambda i: (i, 0)),
            pl.BlockSpec((pw + aw, d), lambda i: (0, 0)),
        ],
        out_specs=pl.BlockSpec((tm, d), lambda i: (i, 0)),
        out_shape=jax.ShapeDtypeStruct((n_rows, d), F32),
        scratch_shapes=[pltpu.VMEM((tm, d), F32), pltpu.VMEM((3, SUBLANES, d), F32)],
        compiler_params=_params("arbitrary"),
        name="odd_mixer_out",
    )(x, mod, g_post.reshape(1, d), pooled, att, w_out)


def kernel(x, c, ctx, c_ctx, mod_w, mod_b, norm_pre, norm_post, ffn_gate, ffn_up, ffn_down,
           ev_w_in, ev_w_out, lru_conv_w, lru_conv_b, lru_wa, lru_ba, lru_wx, lru_bx, lru_lambda,
           ret_decay_logit, ret_gn, od_w_in, od_w_out, pool_w, pool_scale, q_norm, k_norm):
    B, S, D = x.shape
    Lc = ctx.shape[1]
    depth = mod_w.shape[0]
    assert depth == 2 and B < SUBLANES
    geom = dict(batch=B, seq_len=S, ctx_len=Lc)
    n_lat_rows = B * S

    grid_rows = S // GRID_W
    row = jnp.repeat(jnp.arange(grid_rows, dtype=F32), GRID_W)
    col = jnp.tile(jnp.arange(GRID_W, dtype=F32), grid_rows)
    n_ax = HEAD_DIM // 4
    f_ax = ROPE_THETA ** (-jnp.arange(n_ax, dtype=F32) / n_ax)
    ang2 = jnp.concatenate([row[:, None] * f_ax, col[:, None] * f_ax], axis=-1)
    cos2, sin2 = jnp.cos(ang2), jnp.sin(ang2)
    cos_full = jnp.concatenate([cos2, cos2], axis=-1)
    sin_signed = jnp.concatenate([-sin2, sin2], axis=-1)
    ret_dk = ev_w_out.shape[1] // 2 // RET_HEADS
    n_r = ret_dk // 2
    f_r = RET_THETA ** (-jnp.arange(n_r, dtype=F32) / n_r)
    ang1 = jnp.arange(S, dtype=F32)[:, None] * f_r
    cos1, sin1 = jnp.cos(ang1), jnp.sin(ang1)

    wg, wu, wd = ffn_gate.astype(BF16), ffn_up.astype(BF16), ffn_down.astype(BF16)

    c_all = jnp.zeros((SUBLANES, D), F32).at[:B].set(c).at[B].set(c_ctx)
    mods = modulation_vectors(c_all, mod_w, mod_b).reshape(depth, SUBLANES, 3, 3, D)[:, :B + 1]

    xs = jnp.concatenate([x.reshape(n_lat_rows, D), ctx.reshape(B * Lc, D)], axis=0)
    n_rows = xs.shape[0]

    li, e = 0, 0
    mod = lambda sub: mods[li, :, sub]

    def ffn(rows_in, n, sub, fi):
        return ffn_sublayer(rows_in, n, mod(sub), norm_pre[li, sub], norm_post[li, sub], wg, wu, wd, (li, fi), S)

    xs = ffn(xs, n_rows, 0, 0)
    proj = mixer_in_proj(xs, mod(1), norm_pre[li, 1], ev_w_in[e].astype(BF16), S)
    lru_args = lambda d: (lru_conv_w[e], lru_conv_b[e], lru_wa[e, d].astype(BF16), lru_ba[e, d],
                          lru_wx[e, d].astype(BF16), lru_bx[e, d], lru_lambda[e, d])
    h_fwd = rglru_direction(proj, 1, 0, *lru_args(0), None, **geom)
    lru = rglru_direction(proj, 1, 0, *lru_args(1), h_fwd, **geom)
    log_g = -jax.nn.softplus(-ret_decay_logit[e].astype(F32))
    o_fwd, o_bwd = retention_bidir(proj, 2, 3, 4, log_g, cos1, sin1, **geom)
    xs = even_mixer_out(xs, mod(1), norm_post[li, 1], lru, o_fwd, o_bwd, proj, 5, ret_gn[e],
                        ev_w_out[e].astype(BF16), S)
    xs = ffn(xs, n_rows, 2, 1)

    li, o = 1, 0
    xs = ffn(xs, n_rows, 0, 0)
    proj = mixer_in_proj(xs, mod(1), norm_pre[li, 1], od_w_in[o].astype(BF16), S)
    pw = POOL_GROUP * len(POOL_WINDOWS)
    kvw = KV_HEADS * HEAD_DIM
    q0 = pw // HEAD_DIM
    k_col = (pw + ATT_HEADS * HEAD_DIM) // kvw
    keys, values_t = kv_prepare(proj, k_col, k_col + 1, k_norm[o], cos_full, sin_signed, **geom)
    att = gqa_attention(proj, q0, q_norm[o], cos_full, sin_signed, keys, values_t, **geom)
    pooled = multiscale_pool(proj, pool_w[o].astype(BF16), pool_scale[o], batch=B, seq_len=S)
    xl = odd_mixer_out(xs, n_lat_rows, mod(1), norm_post[li, 1], pooled, att, od_w_out[o].astype(BF16), S)
    xl = ffn(xl, n_lat_rows, 2, 1)
    return xl.reshape(B, S, D)
```

```python
import functools

import jax
import jax.numpy as jnp
from jax import lax
from jax.experimental import pallas as pl
from jax.experimental.pallas import tpu as pltpu

GRID_W = 64
EPS = 1e-6
FFN_STEP = 0.5
LRU_BLOCKS = 8
LRU_C = 8.0
CONV_W = 4
CONV_LEFT = 2
RET_HEADS = 4
RET_CHUNK = 128
RET_THETA = 10000.0
POOL_WINDOWS = (2, 4, 8, 16)
POOL_GROUP = 128
ATT_HEADS = 12
KV_HEADS = 4
GROUP = ATT_HEADS // KV_HEADS
HEAD_DIM = 128
ROPE_THETA = 10000.0

VMEM_LIMIT_BYTES = 56 * 1024 * 1024
SUBLANES = 8

ROW_TILE = 512
PROJ_ROW_TILE = 1024
OUT_ROW_TILE = 256
FFN_FF_TILE = 512
FFN_WINDOW_ROWS = 48
PROJ_COL_TILE = 1024
SEQ_BLOCK = 256
ATT_Q_TILE = 512
ATT_KV_TILE = 1408
POOL_BLOCK = 512
LOG2_E = 1.4426950408889634
NORM_ROWS = 16
NORM_UNROLL = 16

BF16 = jnp.bfloat16
F32 = jnp.float32


def _params(*sem):
    return pltpu.CompilerParams(dimension_semantics=sem, vmem_limit_bytes=VMEM_LIMIT_BYTES)


def _dot(a, b):
    return jnp.dot(a, b, preferred_element_type=F32)


def _modulate_rows(x_ref, gpre_ref, mod_ref, row_s, h_ref):
    _set_modulate_rows(gpre_ref, mod_ref, row_s)

    def body(i, carry):
        _modulate_group(x_ref, row_s, h_ref, i * NORM_ROWS)
        return carry

    lax.fori_loop(0, x_ref.shape[0] // NORM_ROWS, body, 0, unroll=NORM_UNROLL)


def _set_modulate_rows(gpre_ref, mod_ref, row_s):
    d = gpre_ref.shape[1]
    row_s[0] = jnp.broadcast_to(gpre_ref[...] * (1.0 + mod_ref[0, 1:2, :]), (SUBLANES, d))
    row_s[1] = jnp.broadcast_to(mod_ref[0, 0:1, :], (SUBLANES, d))


def _modulate_group(x_ref, row_s, h_ref, start):
    d = x_ref.shape[1]
    r = pl.multiple_of(start, NORM_ROWS)
    x = x_ref[pl.ds(r, NORM_ROWS), :]
    ms = jnp.mean(x * x, axis=-1, keepdims=True)
    xn = (x * lax.rsqrt(ms + EPS)).reshape(NORM_ROWS // SUBLANES, SUBLANES, d)
    h = xn * row_s[0] + row_s[1]
    h_ref[pl.ds(r, NORM_ROWS), :] = h.reshape(NORM_ROWS, d).astype(h_ref.dtype)


def _gated_residual_rows(x_ref, y_ref, gpost_ref, mod_ref, row_s, o_ref, step):
    _set_residual_row(gpost_ref, mod_ref, row_s, step)

    def body(i, carry):
        _residual_group(x_ref, y_ref, row_s, o_ref, i * NORM_ROWS)
        return carry

    lax.fori_loop(0, x_ref.shape[0] // NORM_ROWS, body, 0, unroll=NORM_UNROLL)


def _set_residual_row(gpost_ref, mod_ref, row_s, step):
    d = gpost_ref.shape[1]
    row_s[2] = jnp.broadcast_to((step * mod_ref[0, 2:3, :]) * gpost_ref[...], (SUBLANES, d))


def _residual_group(x_ref, y_ref, row_s, o_ref, start):
    d = x_ref.shape[1]
    r = pl.multiple_of(start, NORM_ROWS)
    y = y_ref[pl.ds(r, NORM_ROWS), :]
    ms = jnp.mean(y * y, axis=-1, keepdims=True)
    yn = (y * lax.rsqrt(ms + EPS)).reshape(NORM_ROWS // SUBLANES, SUBLANES, d)
    o_ref[pl.ds(r, NORM_ROWS), :] = x_ref[pl.ds(r, NORM_ROWS), :] + (yn * row_s[2]).reshape(NORM_ROWS, d)


def _mod_spec(d, tm, seq_len, n_groups, tile_offset=0):
    return pl.BlockSpec(
        (1, 3, d), lambda i, *_: (jnp.minimum(((i + tile_offset) * tm) // seq_len, n_groups - 1), 0, 0))


def _mod_kernel(c_ref, w_ref, b_ref, o_ref):
    c = c_ref[...]
    sc = c * jax.nn.sigmoid(c)
    o_ref[0] = _dot(sc.astype(BF16), w_ref[0].astype(BF16)) + b_ref[0]


def modulation_vectors(c_all, mod_w, mod_b):
    depth, d, n = mod_w.shape
    tn = PROJ_COL_TILE
    return pl.pallas_call(
        _mod_kernel,
        grid=(depth, n // tn),
        in_specs=[
            pl.BlockSpec((SUBLANES, d), lambda l, j: (0, 0)),
            pl.BlockSpec((1, d, tn), lambda l, j: (l, 0, j)),
            pl.BlockSpec((1, 1, tn), lambda l, j: (l, 0, j)),
        ],
        out_specs=pl.BlockSpec((1, SUBLANES, tn), lambda l, j: (l, 0, j)),
        out_shape=jax.ShapeDtypeStruct((depth, SUBLANES, n), F32),
        compiler_params=_params("arbitrary", "arbitrary"),
        name="modulation_vectors",
    )(c_all, mod_w, mod_b.reshape(depth, 1, n))


def _ffn_kernel(xn_ref, xp_ref, modn_ref, modp_ref, gpre_ref, gpost_ref, wg_ref, wu_ref, wd_ref, o_ref,
                h0_ref, h1_ref, acc0_ref, acc1_ref, row_s, *, n_grid_rows):
    t, j = pl.program_id(0), pl.program_id(1)
    n_tiles = n_grid_rows - 1
    tm = xn_ref.shape[0]
    start = jnp.minimum(j * FFN_WINDOW_ROWS, tm - FFN_WINDOW_ROWS)

    @pl.when(jnp.logical_and(t == 0, j == 0))
    def _():
        _modulate_rows(xp_ref, gpre_ref, modp_ref, row_s, h0_ref)
        acc0_ref[...] = jnp.zeros_like(acc0_ref)
        acc1_ref[...] = jnp.zeros_like(acc1_ref)

    @pl.when(j == 0)
    def _():
        _set_modulate_rows(gpre_ref, modn_ref, row_s)
        _set_residual_row(gpost_ref, modp_ref, row_s, FFN_STEP)

    def neighbours(h_next, acc_prev):
        for k in range(FFN_WINDOW_ROWS // NORM_ROWS):
            if h_next is not None:
                _modulate_group(xn_ref, row_s, h_next, start + k * NORM_ROWS)
            _residual_group(xp_ref, acc_prev, row_s, o_ref, start + k * NORM_ROWS)

    for parity, (h_cur, h_next, acc_cur, acc_prev) in enumerate(
            ((h0_ref, h1_ref, acc0_ref, acc1_ref), (h1_ref, h0_ref, acc1_ref, acc0_ref))):
        @pl.when(jnp.logical_and(t < n_tiles, lax.rem(t, 2) == parity))
        def _(h_cur=h_cur, h_next=h_next, acc_cur=acc_cur, acc_prev=acc_prev):
            h = h_cur[...]
            g = _dot(h, wg_ref[...])
            u = _dot(h, wu_ref[...])
            a = (g * jax.nn.sigmoid(g)) * u
            acc = jnp.where(j == 0, 0.0, acc_cur[...])
            acc_cur[...] = acc + _dot(a.astype(BF16), wd_ref[...])
            neighbours(h_next, acc_prev)

    @pl.when(t == n_tiles)
    def _():
        neighbours(None, acc1_ref if n_tiles % 2 == 0 else acc0_ref)


def _ffn_kernel_into(*refs, n_grid_rows):
    _ffn_kernel(*refs[:9], *refs[10:], n_grid_rows=n_grid_rows)


def ffn_sublayer(x, n_rows, mod, g_pre, g_post, w_gate, w_up, w_down, which, seq_len,
                 out_rows=None, tile_offset=0, into=None):
    d = x.shape[1]
    f = w_gate.shape[-1]
    tm, tf = ROW_TILE, FFN_FF_TILE
    li, fi = which
    out_rows = n_rows if out_rows is None else out_rows
    n_tiles, n_steps = n_rows // tm, f // tf
    assert n_rows % tm == 0 and f % tf == 0 and seq_len % tm == 0 and out_rows >= tile_offset * tm + n_rows
    assert FFN_WINDOW_ROWS % NORM_ROWS == 0 and n_steps * FFN_WINDOW_ROWS >= tm
    nxt = lambda t: jnp.minimum(t + 1, n_tiles - 1)
    prv = lambda t: jnp.clip(t - 1, 0, n_tiles - 1)
    group = lambda tile: jnp.minimum(((tile + tile_offset) * tm) // seq_len, mod.shape[0] - 1)
    col = lambda t, j: jnp.where(t == n_tiles, 0, j)
    in_specs = [
        pl.BlockSpec((tm, d), lambda t, j: (nxt(t), 0)),
        pl.BlockSpec((tm, d), lambda t, j: (prv(t), 0)),
        pl.BlockSpec((1, 3, d), lambda t, j: (group(nxt(t)), 0, 0)),
        pl.BlockSpec((1, 3, d), lambda t, j: (group(prv(t)), 0, 0)),
        pl.BlockSpec((1, d), lambda t, j: (0, 0)),
        pl.BlockSpec((1, d), lambda t, j: (0, 0)),
        pl.BlockSpec((None, None, d, tf), lambda t, j: (li, fi, 0, col(t, j))),
        pl.BlockSpec((None, None, d, tf), lambda t, j: (li, fi, 0, col(t, j))),
        pl.BlockSpec((None, None, tf, d), lambda t, j: (li, fi, col(t, j), 0)),
    ]
    args = [x, x, mod, mod, g_pre.reshape(1, d), g_post.reshape(1, d), w_gate, w_up, w_down]
    aliases = {}
    if into is not None:
        assert into.shape == (out_rows, d)
        in_specs.append(pl.BlockSpec(memory_space=pl.ANY))
        aliases = {len(args): 0}
        args.append(into)
    return pl.pallas_call(
        functools.partial(_ffn_kernel if into is None else _ffn_kernel_into, n_grid_rows=n_tiles + 1),
        grid=(n_tiles + 1, n_steps),
        in_specs=in_specs,
        out_specs=pl.BlockSpec((tm, d), lambda t, j: (prv(t) + tile_offset, 0)),
        out_shape=jax.ShapeDtypeStruct((out_rows, d), F32),
        input_output_aliases=aliases,
        scratch_shapes=[pltpu.VMEM((tm, d), BF16), pltpu.VMEM((tm, d), BF16), pltpu.VMEM((tm, d), F32),
                        pltpu.VMEM((tm, d), F32), pltpu.VMEM((3, SUBLANES, d), F32)],
        compiler_params=_params("arbitrary", "arbitrary"),
        name="ffn_sublayer",
    )(*args)


def _in_proj_kernel(x_ref, mod_ref, gpre_ref, w_ref, o_ref, h_ref, row_s):
    @pl.when(pl.program_id(1) == 0)
    def _():
        _modulate_rows(x_ref, gpre_ref, mod_ref, row_s, h_ref)

    o_ref[...] = _dot(h_ref[...], w_ref[...])


def mixer_in_proj(x, mod, g_pre, w_in, seq_len):
    rows, d = x.shape
    n = w_in.shape[1]
    tm, tn = PROJ_ROW_TILE, PROJ_COL_TILE
    assert rows % tm == 0 and n % tn == 0 and seq_len % tm == 0
    return pl.pallas_call(
        _in_proj_kernel,
        grid=(rows // tm, n // tn),
        in_specs=[
            pl.BlockSpec((tm, d), lambda i, j: (i, 0)),
            _mod_spec(d, tm, seq_len, mod.shape[0]),
            pl.BlockSpec((1, d), lambda i, j: (0, 0)),
            pl.BlockSpec((d, tn), lambda i, j: (0, j)),
        ],
        out_specs=pl.BlockSpec((tm, tn), lambda i, j: (i, j)),
        out_shape=jax.ShapeDtypeStruct((rows, n), F32),
        scratch_shapes=[pltpu.VMEM((tm, d), BF16), pltpu.VMEM((3, SUBLANES, d), F32)],
        compiler_params=_params("arbitrary", "arbitrary"),
        name="mixer_in_proj",
    )(x, mod, g_pre.reshape(1, d), w_in)


def _lru_kernel(*refs, reverse, n_lat, final):
    if final:
        (r_ref, rp_ref, rn_ref, cw_ref, cb_ref, wa_ref, ba_ref, wx_ref, bx_ref, lam_ref,
         hf_ref, g_ref, o_ref, a_s, b_s, h_s, hs_s) = refs
    else:
        (r_ref, rp_ref, rn_ref, cw_ref, cb_ref, wa_ref, ba_ref, wx_ref, bx_ref, lam_ref,
         o_ref, a_s, b_s, h_s) = refs
        hs_s = o_ref
    s = pl.program_id(1)
    tb, w = r_ref.shape
    is_lat = s > 0
    k = (n_lat - s) if reverse else (s - 1)
    has_prev = jnp.logical_and(is_lat, k > 0)
    has_next = jnp.logical_and(is_lat, k < n_lat - 1)

    prev = jnp.where(has_prev, rp_ref[...], 0.0)
    nxt = jnp.where(has_next, rn_ref[...], 0.0)
    ext = jnp.concatenate([prev, r_ref[...], nxt], axis=0)
    base = SUBLANES - CONV_LEFT
    u = ext[base:base + tb] * cw_ref[0:1, :]
    for t in range(1, CONV_W):
        u = u + ext[base + t:base + t + tb] * cw_ref[t:t + 1, :]
    u = u + cb_ref[...]

    ub = u.astype(BF16)
    blk = w // LRU_BLOCKS
    ra = jnp.concatenate([_dot(ub[:, n * blk:(n + 1) * blk], wa_ref[n]) for n in range(LRU_BLOCKS)], axis=1)
    xa = jnp.concatenate([_dot(ub[:, n * blk:(n + 1) * blk], wx_ref[n]) for n in range(LRU_BLOCKS)], axis=1)
    r_gate = jax.nn.sigmoid(ra + ba_ref[...])
    i_gate = jax.nn.sigmoid(xa + bx_ref[...])
    lam = lam_ref[...]
    softplus_neg_lam = jnp.maximum(-lam, 0.0) + jnp.log1p(jnp.exp(-jnp.abs(lam)))
    log_a = -LRU_C * r_gate * softplus_neg_lam
    a = jnp.exp(log_a)
    one_minus_a2 = -jnp.tanh(log_a) * (a * a + 1.0)
    a_s[...] = a
    b_s[...] = jnp.sqrt(one_minus_a2) * (i_gate * u)

    @pl.when(s == 0)
    def _():
        h_s[...] = jnp.zeros_like(h_s)

    def body(t, h):
        tt = (tb - 1 - t) if reverse else t
        h = a_s[pl.ds(tt, 1), :] * h + b_s[pl.ds(tt, 1), :]
        hs_s[pl.ds(tt, 1), :] = h
        return h

    h_s[...] = lax.fori_loop(0, tb, body, h_s[...], unroll=8)

    if final:
        o_ref[...] = (jax.nn.gelu(g_ref[...]) * (hf_ref[...] + hs_s[...])).astype(o_ref.dtype)


def rglru_direction(proj, r_col, g_col, conv_w, conv_b, wa, ba, wx, bx, lam, h_fwd, *, batch, seq_len, ctx_len):
    rows = proj.shape[0]
    w = wa.shape[0] * wa.shape[1]
    tb = SEQ_BLOCK
    assert ctx_len == tb and seq_len % tb == 0
    n_lat = seq_len // tb
    ctx0 = batch * n_lat
    hb = tb // SUBLANES
    n_halo = rows // SUBLANES
    final = h_fwd is not None

    def blk(b, s):
        k = (n_lat - s) if final else (s - 1)
        return jnp.where(s == 0, ctx0 + b, b * n_lat + k)

    row_spec = lambda col: pl.BlockSpec((tb, w), lambda b, s: (blk(b, s), col))
    vec = lambda n: pl.BlockSpec((n, w), lambda b, s: (0, 0))
    mat = pl.BlockSpec(wa.shape, lambda b, s: (0, 0, 0))
    in_specs = [
        row_spec(r_col),
        pl.BlockSpec((SUBLANES, w), lambda b, s: (jnp.maximum(blk(b, s) * hb - 1, 0), r_col)),
        pl.BlockSpec((SUBLANES, w), lambda b, s: (jnp.minimum((blk(b, s) + 1) * hb, n_halo - 1), r_col)),
        vec(CONV_W), vec(1), mat, vec(1), mat, vec(1), vec(1),
    ]
    args = [proj, proj, proj, conv_w, conv_b.reshape(1, w), wa, ba.reshape(1, w), wx, bx.reshape(1, w),
            lam.reshape(1, w)]
    scratch = [pltpu.VMEM((tb, w), F32), pltpu.VMEM((tb, w), F32), pltpu.VMEM((1, w), F32)]
    if final:
        in_specs += [row_spec(0), row_spec(g_col)]
        args += [h_fwd, proj]
        scratch.append(pltpu.VMEM((tb, w), F32))
    return pl.pallas_call(
        functools.partial(_lru_kernel, reverse=final, n_lat=n_lat, final=final),
        grid=(batch, n_lat + 1),
        in_specs=in_specs,
        out_specs=row_spec(0),
        out_shape=jax.ShapeDtypeStruct((rows, w), BF16 if final else F32),
        scratch_shapes=scratch,
        compiler_params=_params("arbitrary", "arbitrary"),
        name="rglru_reverse" if final else "rglru_forward",
    )(*args)


def _rotate_halves(x, cos, sin):
    half = x.shape[-1] // 2
    x1, x2 = x[:, :half], x[:, half:]
    return jnp.concatenate([x1 * cos - x2 * sin, x1 * sin + x2 * cos], axis=-1)


def _retention_kernel(lg_ref, qf_ref, kf_ref, vf_ref, cf_ref, sf_ref, qb_ref, kb_ref, vb_ref, cb_ref, sb_ref,
                      of_ref, ob_ref, state, *, n_ctx):
    s = pl.program_id(1)
    c, width = qf_ref.shape
    dk = width // RET_HEADS
    is_lat = s >= n_ctx

    @pl.when(s == 0)
    def _():
        state[...] = jnp.zeros_like(state)

    ii = lax.broadcasted_iota(jnp.int32, (c, 1), 0).astype(F32)
    diff = ii - lax.broadcasted_iota(jnp.int32, (1, c), 1).astype(F32)
    k_scale = dk ** -0.5
    dirs = ((qf_ref, kf_ref, vf_ref, cf_ref, sf_ref, of_ref), (qb_ref, kb_ref, vb_ref, cb_ref, sb_ref, ob_ref))
    for d, (q_ref, k_ref, v_ref, cos_ref, sin_ref, o_ref) in enumerate(dirs):
        cos = jnp.where(is_lat, cos_ref[...], 1.0)
        sin = jnp.where(is_lat, sin_ref[...], 0.0)
        for h in range(RET_HEADS):
            lg = lg_ref[d, h]
            sl = slice(h * dk, (h + 1) * dk)
            q = _rotate_halves(q_ref[:, sl], cos, sin)
            k = _rotate_halves(k_ref[:, sl], cos, sin) * k_scale
            vb = v_ref[:, sl].astype(BF16)
            if d == 0:
                intra = jnp.where(diff >= 0, jnp.exp(lg * jnp.maximum(diff, 0.0)), 0.0)
                q_dec = jnp.exp(lg * (ii + 1.0))
                k_dec = jnp.exp(lg * (c - 1.0 - ii))
            else:
                intra = jnp.where(diff <= 0, jnp.exp(lg * jnp.maximum(-diff, 0.0)), 0.0)
                q_dec = jnp.exp(lg * (c - ii))
                k_dec = jnp.exp(lg * ii)
            s_dec = jnp.exp(lg * jnp.full((1, 1), float(c), F32))
            qb = q.astype(BF16)
            scores = lax.dot_general(qb, k.astype(BF16), (((1,), (1,)), ((), ())), preferred_element_type=F32)
            scores = scores * intra
            st = state[d, h]
            o = _dot(scores.astype(BF16), vb) + _dot((q * q_dec).astype(BF16), st.astype(BF16))
            o_ref[:, sl] = o
            kd_t = (k * k_dec).T.astype(BF16)
            state[d, h] = st * s_dec + _dot(kd_t, vb)


def retention_bidir(proj, q_col, k_col, v_col, log_g, cos, sin, *, batch, seq_len, ctx_len):
    rows = proj.shape[0]
    c = RET_CHUNK
    width = RET_HEADS * 2 * cos.shape[1]
    dk = width // RET_HEADS
    n_lat, n_ctx = seq_len // c, ctx_len // c
    ctx0 = batch * n_lat
    last = n_lat + n_ctx - 1

    def fwd(b, s):
        return jnp.where(s < n_ctx, ctx0 + b * n_ctx + s, b * n_lat + s - n_ctx)

    def bwd(b, s):
        return jnp.where(s < n_ctx, ctx0 + b * n_ctx + (n_ctx - 1 - s), b * n_lat + (last - s))

    def specs(idx):
        tab = pl.BlockSpec((c, dk // 2), lambda b, s: (jnp.where(s >= n_ctx, idx(b, s) - b * n_lat, 0), 0))
        return [pl.BlockSpec((c, width), lambda b, s, col=col: (idx(b, s), col)) for col in (q_col, k_col, v_col)] + [tab, tab]

    out_spec = lambda idx: pl.BlockSpec((c, width), lambda b, s: (idx(b, s), 0))
    return pl.pallas_call(
        functools.partial(_retention_kernel, n_ctx=n_ctx),
        grid=(batch, n_lat + n_ctx),
        in_specs=[pl.BlockSpec(memory_space=pltpu.SMEM)] + specs(fwd) + specs(bwd),
        out_specs=[out_spec(fwd), out_spec(bwd)],
        out_shape=[jax.ShapeDtypeStruct((rows, width), F32)] * 2,
        scratch_shapes=[pltpu.VMEM((2, RET_HEADS, dk, dk), F32)],
        compiler_params=_params("arbitrary", "arbitrary"),
        name="retention_bidir",
    )(log_g, proj, proj, proj, cos, sin, proj, proj, proj, cos, sin)


def _even_out_kernel(x_ref, mod_ref, gpost_ref, lru_ref, of_ref, ob_ref, gate_ref, gn_ref, w_ref, o_ref, y_s, row_s):
    o = of_ref[...] + ob_ref[...]
    width = o.shape[1]
    dv = width // RET_HEADS
    parts = []
    for h in range(RET_HEADS):
        oh = o[:, h * dv:(h + 1) * dv]
        mu = jnp.mean(oh, axis=-1, keepdims=True)
        cen = oh - mu
        var = jnp.mean(cen * cen, axis=-1, keepdims=True)
        parts.append(cen * lax.rsqrt(var + EPS))
    gate = gate_ref[...]
    ret = jnp.concatenate(parts, axis=-1) * gn_ref[...] * (gate * jax.nn.sigmoid(gate))
    lw = lru_ref.shape[1]
    y_s[...] = _dot(lru_ref[...], w_ref[0:lw, :]) + _dot(ret.astype(BF16), w_ref[lw:lw + width, :])
    _gated_residual_rows(x_ref, y_s, gpost_ref, mod_ref, row_s, o_ref, 1.0)


def even_mixer_out(x, mod, g_post, lru, o_fwd, o_bwd, proj, gate_col, gn_g, w_out, seq_len):
    rows, d = x.shape
    tm = OUT_ROW_TILE
    lw, rw = lru.shape[1], o_fwd.shape[1]
    return pl.pallas_call(
        _even_out_kernel,
        grid=(rows // tm,),
        in_specs=[
            pl.BlockSpec((tm, d), lambda i: (i, 0)),
            _mod_spec(d, tm, seq_len, mod.shape[0]),
            pl.BlockSpec((1, d), lambda i: (0, 0)),
            pl.BlockSpec((tm, lw), lambda i: (i, 0)),
            pl.BlockSpec((tm, rw), lambda i: (i, 0)),
            pl.BlockSpec((tm, rw), lambda i: (i, 0)),
            pl.BlockSpec((tm, rw), lambda i: (i, gate_col)),
            pl.BlockSpec((1, rw), lambda i: (0, 0)),
            pl.BlockSpec((lw + rw, d), lambda i: (0, 0)),
        ],
        out_specs=pl.BlockSpec((tm, d), lambda i: (i, 0)),
        out_shape=jax.ShapeDtypeStruct((rows, d), F32),
        scratch_shapes=[pltpu.VMEM((tm, d), F32), pltpu.VMEM((3, SUBLANES, d), F32)],
        compiler_params=_params("arbitrary"),
        name="even_mixer_out",
    )(x, mod, g_post.reshape(1, d), lru, o_fwd, o_bwd, proj, gn_g.reshape(1, rw), w_out)


def _head_norm_rotary(x, g, cos_full, sin_signed):
    ms = jnp.mean(x * x, axis=-1, keepdims=True)
    y = x * lax.rsqrt(ms + EPS) * g
    return y * cos_full + pltpu.roll(y, HEAD_DIM // 2, axis=1) * sin_signed


def _kv_prep_kernel(k_ref, v_ref, kg_ref, cos_ref, sin_ref, ko_ref, vo_ref):
    is_lat = pl.program_id(1) > 0
    cos = jnp.where(is_lat, cos_ref[...], 1.0)
    sin = jnp.where(is_lat, sin_ref[...], 0.0)
    for h in range(KV_HEADS):
        sl = slice(h * HEAD_DIM, (h + 1) * HEAD_DIM)
        ko_ref[:, sl] = _head_norm_rotary(k_ref[:, sl], kg_ref[...], cos, sin).astype(BF16)
        vo_ref[sl, :] = v_ref[:, sl].T.astype(BF16)


def kv_prepare(proj, k_col, v_col, k_g, cos_full, sin_signed, *, batch, seq_len, ctx_len):
    tb = SEQ_BLOCK
    kvw = KV_HEADS * HEAD_DIM
    assert ctx_len == tb
    n_lat = seq_len // tb
    ctx0 = batch * n_lat
    n_keys = seq_len + ctx_len
    src = lambda col: pl.BlockSpec((tb, kvw), lambda b, s: (jnp.where(s == 0, ctx0 + b, b * n_lat + s - 1), col))
    tab = pl.BlockSpec((tb, HEAD_DIM), lambda b, s: (jnp.maximum(s - 1, 0), 0))
    return pl.pallas_call(
        _kv_prep_kernel,
        grid=(batch, n_lat + 1),
        in_specs=[src(k_col), src(v_col), pl.BlockSpec((1, HEAD_DIM), lambda b, s: (0, 0)), tab, tab],
        out_specs=[pl.BlockSpec((tb, kvw), lambda b, s: (b * (n_lat + 1) + s, 0)),
                   pl.BlockSpec((kvw, tb), lambda b, s: (b, s))],
        out_shape=[jax.ShapeDtypeStruct((batch * n_keys, kvw), BF16),
                   jax.ShapeDtypeStruct((batch * kvw, n_keys), BF16)],
        compiler_params=_params("arbitrary", "arbitrary"),
        name="kv_prepare",
    )(proj, proj, k_g.reshape(1, HEAD_DIM), cos_full, sin_signed)


def _attention_kernel(q0_ref, q1_ref, q2_ref, qg_ref, cos_ref, sin_ref, k_ref, vt_ref, o_ref, s_ref):
    tq = q0_ref.shape[0]
    n_keys = k_ref.shape[0]
    tk = ATT_KV_TILE
    cos, sin = cos_ref[...], sin_ref[...]
    c = HEAD_DIM ** -0.5 * LOG2_E
    q_t = [(_head_norm_rotary(r[...], qg_ref[...], cos, sin) * c).T.astype(BF16) for r in (q0_ref, q1_ref, q2_ref)]

    n_chunks = n_keys // tk

    def scores(j, g):
        start = pl.multiple_of(j * tk, tk)
        return _dot(k_ref[pl.ds(start, tk), :], q_t[g])

    def update(j, s, state):
        m, l, acc = state
        start = pl.multiple_of(j * tk, tk)
        m_new = jnp.maximum(m, jnp.max(s, axis=0, keepdims=True))
        alpha = jnp.exp2(m - m_new)
        p = jnp.exp2(s - m_new)
        l = alpha * l + jnp.sum(p, axis=0, keepdims=True)
        acc = alpha * acc + _dot(vt_ref[:, pl.ds(start, tk)], p.astype(BF16))
        return m_new, l, acc

    s_ref[...] = scores(0, 0)

    def body(j, carry):
        state = list(carry)
        s_cur = s_ref[...]
        for g in range(GROUP):
            s_next = scores(j, g + 1) if g + 1 < GROUP else scores(jnp.minimum(j + 1, n_chunks - 1), 0)
            state[g] = update(j, s_cur, state[g])
            s_cur = s_next
        s_ref[...] = s_cur
        return tuple(state)

    init = tuple((jnp.full((1, tq), -jnp.inf, F32), jnp.zeros((1, tq), F32), jnp.zeros((HEAD_DIM, tq), F32))
                 for _ in range(GROUP))
    final = lax.fori_loop(0, n_chunks, body, init, unroll=True)
    for g, (_, l, acc) in enumerate(final):
        o_ref[:, g * HEAD_DIM:(g + 1) * HEAD_DIM] = (acc / l).T.astype(o_ref.dtype)


def gqa_attention(proj, q_col0, q_g, cos_full, sin_signed, keys, values_t, *, batch, seq_len, ctx_len):
    tq = ATT_Q_TILE
    n_keys = seq_len + ctx_len
    assert seq_len % tq == 0 and n_keys % ATT_KV_TILE == 0
    nq = seq_len // tq
    qspec = lambda g: pl.BlockSpec((tq, HEAD_DIM), lambda b, kh, i: (b * nq + i, q_col0 + kh * GROUP + g))
    tab = pl.BlockSpec((tq, HEAD_DIM), lambda b, kh, i: (i, 0))
    return pl.pallas_call(
        _attention_kernel,
        grid=(batch, KV_HEADS, nq),
        in_specs=[qspec(0), qspec(1), qspec(2), pl.BlockSpec((1, HEAD_DIM), lambda b, kh, i: (0, 0)), tab, tab,
                  pl.BlockSpec((n_keys, HEAD_DIM), lambda b, kh, i: (b, kh)),
                  pl.BlockSpec((HEAD_DIM, n_keys), lambda b, kh, i: (b * KV_HEADS + kh, 0))],
        out_specs=pl.BlockSpec((tq, GROUP * HEAD_DIM), lambda b, kh, i: (b * nq + i, kh)),
        out_shape=jax.ShapeDtypeStruct((batch * seq_len, ATT_HEADS * HEAD_DIM), BF16),
        scratch_shapes=[pltpu.VMEM((ATT_KV_TILE, tq), F32)],
        compiler_params=_params("arbitrary", "arbitrary", "arbitrary"),
        name="gqa_attention",
    )(proj, proj, proj, q_g.reshape(1, HEAD_DIM), cos_full, sin_signed, keys, values_t)


def _pool_kernel(x_ref, xp_ref, xn_ref, w_ref, scale_ref, o_ref, *, seq_len):
    i = pl.program_id(1)
    tb = x_ref.shape[0]
    x = x_ref[...]
    prev = jnp.where(i > 0, xp_ref[...], 0.0)
    nxt = jnp.where(i < pl.num_programs(1) - 1, xn_ref[...], 0.0)
    ext = jnp.concatenate([prev, x, nxt], axis=0)
    t = i * tb + lax.broadcasted_iota(jnp.int32, (tb, 1), 0)
    for gi, win in enumerate(POOL_WINDOWS):
        sl = slice(gi * POOL_GROUP, (gi + 1) * POOL_GROUP)
        half = win // 2
        e = ext[:, sl]
        tot = e[SUBLANES - half:SUBLANES - half + tb]
        for off in range(1 - half, half):
            tot = tot + e[SUBLANES + off:SUBLANES + off + tb]
        cnt = (jnp.minimum(t + half, seq_len) - jnp.maximum(t - half, 0)).astype(F32)
        centred = tot / cnt - x[:, sl]
        o_ref[:, sl] = (_dot(centred.astype(BF16), w_ref[gi]) * scale_ref[:, sl]).astype(o_ref.dtype)


def multiscale_pool(proj, pool_w, pool_scale, *, batch, seq_len):
    tb = POOL_BLOCK
    pw = POOL_GROUP * len(POOL_WINDOWS)
    nb = seq_len // tb
    hb = tb // SUBLANES
    assert max(POOL_WINDOWS) // 2 <= SUBLANES
    return pl.pallas_call(
        functools.partial(_pool_kernel, seq_len=seq_len),
        grid=(batch, nb),
        in_specs=[
            pl.BlockSpec((tb, pw), lambda b, i: (b * nb + i, 0)),
            pl.BlockSpec((SUBLANES, pw), lambda b, i: (jnp.maximum((b * nb + i) * hb - 1, 0), 0)),
            pl.BlockSpec((SUBLANES, pw), lambda b, i: ((b * nb + i + 1) * hb, 0)),
            pl.BlockSpec(pool_w.shape, lambda b, i: (0, 0, 0)),
            pl.BlockSpec((1, pw), lambda b, i: (0, 0)),
        ],
        out_specs=pl.BlockSpec((tb, pw), lambda b, i: (b * nb + i, 0)),
        out_shape=jax.ShapeDtypeStruct((batch * seq_len, pw), BF16),
        compiler_params=_params("arbitrary", "arbitrary"),
        name="multiscale_pool",
    )(proj, proj, proj, pool_w, pool_scale.reshape(1, pw))


def _odd_out_kernel(x_ref, mod_ref, gpost_ref, pool_ref, att_ref, w_ref, o_ref, y_s, row_s):
    pw = pool_ref.shape[1]
    y_s[...] = _dot(pool_ref[...], w_ref[0:pw, :]) + _dot(att_ref[...], w_ref[pw:, :])
    _gated_residual_rows(x_ref, y_s, gpost_ref, mod_ref, row_s, o_ref, 1.0)


def odd_mixer_out(x, n_rows, mod, g_post, pooled, att, w_out, seq_len):
    d = x.shape[1]
    tm = OUT_ROW_TILE
    pw, aw = pooled.shape[1], att.shape[1]
    return pl.pallas_call(
        _odd_out_kernel,
        grid=(n_rows // tm,),
        in_specs=[
            pl.BlockSpec((tm, d), lambda i: (i, 0)),
            _mod_spec(d, tm, seq_len, mod.shape[0]),
            pl.BlockSpec((1, d), lambda i: (0, 0)),
            pl.BlockSpec((tm, pw), lambda i: (i, 0)),
            pl.BlockSpec((tm, aw), lambda i: (i, 0)),
            pl.BlockSpec((pw + aw, d), lambda i: (0, 0)),
        ],
        out_specs=pl.BlockSpec((tm, d), lambda i: (i, 0)),
        out_shape=jax.ShapeDtypeStruct((n_rows, d), F32),
        scratch_shapes=[pltpu.VMEM((tm, d), F32), pltpu.VMEM((3, SUBLANES, d), F32)],
        compiler_params=_params("arbitrary"),
        name="odd_mixer_out",
    )(x, mod, g_post.reshape(1, d), pooled, att, w_out)


def kernel(x, c, ctx, c_ctx, mod_w, mod_b, norm_pre, norm_post, ffn_gate, ffn_up, ffn_down,
           ev_w_in, ev_w_out, lru_conv_w, lru_conv_b, lru_wa, lru_ba, lru_wx, lru_bx, lru_lambda,
           ret_decay_logit, ret_gn, od_w_in, od_w_out, pool_w, pool_scale, q_norm, k_norm):
    B, S, D = x.shape
    Lc = ctx.shape[1]
    depth = mod_w.shape[0]
    assert depth == 2 and B < SUBLANES
    geom = dict(batch=B, seq_len=S, ctx_len=Lc)
    n_lat_rows = B * S

    grid_rows = S // GRID_W
    row = jnp.repeat(jnp.arange(grid_rows, dtype=F32), GRID_W)
    col = jnp.tile(jnp.arange(GRID_W, dtype=F32), grid_rows)
    n_ax = HEAD_DIM // 4
    f_ax = ROPE_THETA ** (-jnp.arange(n_ax, dtype=F32) / n_ax)
    ang2 = jnp.concatenate([row[:, None] * f_ax, col[:, None] * f_ax], axis=-1)
    cos2, sin2 = jnp.cos(ang2), jnp.sin(ang2)
    cos_full = jnp.concatenate([cos2, cos2], axis=-1)
    sin_signed = jnp.concatenate([-sin2, sin2], axis=-1)
    ret_dk = ev_w_out.shape[1] // 2 // RET_HEADS
    n_r = ret_dk // 2
    f_r = RET_THETA ** (-jnp.arange(n_r, dtype=F32) / n_r)
    ang1 = jnp.arange(S, dtype=F32)[:, None] * f_r
    cos1, sin1 = jnp.cos(ang1), jnp.sin(ang1)

    wg, wu, wd = ffn_gate.astype(BF16), ffn_up.astype(BF16), ffn_down.astype(BF16)

    c_all = jnp.zeros((SUBLANES, D), F32).at[:B].set(c).at[B].set(c_ctx)
    mods = modulation_vectors(c_all, mod_w, mod_b).reshape(depth, SUBLANES, 3, 3, D)[:, :B + 1]

    n_ctx_rows = B * Lc
    n_rows = n_lat_rows + n_ctx_rows

    li, e = 0, 0
    mod = lambda sub: mods[li, :, sub]

    def ffn(rows_in, n, sub, fi, **kw):
        return ffn_sublayer(rows_in, n, mod(sub), norm_pre[li, sub], norm_post[li, sub], wg, wu, wd, (li, fi), S,
                            **kw)

    xs = ffn(x.reshape(n_lat_rows, D), n_lat_rows, 0, 0, out_rows=n_rows)
    xs = ffn(ctx.reshape(n_ctx_rows, D), n_ctx_rows, 0, 0, out_rows=n_rows, tile_offset=n_lat_rows // ROW_TILE,
             into=xs)
    proj = mixer_in_proj(xs, mod(1), norm_pre[li, 1], ev_w_in[e].astype(BF16), S)
    lru_args = lambda d: (lru_conv_w[e], lru_conv_b[e], lru_wa[e, d].astype(BF16), lru_ba[e, d],
                          lru_wx[e, d].astype(BF16), lru_bx[e, d], lru_lambda[e, d])
    h_fwd = rglru_direction(proj, 1, 0, *lru_args(0), None, **geom)
    lru = rglru_direction(proj, 1, 0, *lru_args(1), h_fwd, **geom)
    log_g = -jax.nn.softplus(-ret_decay_logit[e].astype(F32))
    o_fwd, o_bwd = retention_bidir(proj, 2, 3, 4, log_g, cos1, sin1, **geom)
    xs = even_mixer_out(xs, mod(1), norm_post[li, 1], lru, o_fwd, o_bwd, proj, 5, ret_gn[e],
                        ev_w_out[e].astype(BF16), S)
    xs = ffn(xs, n_rows, 2, 1)

    li, o = 1, 0
    xs = ffn(xs, n_rows, 0, 0)
    proj = mixer_in_proj(xs, mod(1), norm_pre[li, 1], od_w_in[o].astype(BF16), S)
    pw = POOL_GROUP * len(POOL_WINDOWS)
    kvw = KV_HEADS * HEAD_DIM
    q0 = pw // HEAD_DIM
    k_col = (pw + ATT_HEADS * HEAD_DIM) // kvw
    keys, values_t = kv_prepare(proj, k_col, k_col + 1, k_norm[o], cos_full, sin_signed, **geom)
    att = gqa_attention(proj, q0, q_norm[o], cos_full, sin_signed, keys, values_t, **geom)
    pooled = multiscale_pool(proj, pool_w[o].astype(BF16), pool_scale[o], batch=B, seq_len=S)
    xl = odd_mixer_out(xs, n_lat_rows, mod(1), norm_post[li, 1], pooled, att, od_w_out[o].astype(BF16), S)
    xl = ffn(xl, n_lat_rows, 2, 1)
    return xl.reshape(B, S, D)
```

```python
import functools

import jax
import jax.numpy as jnp
from jax import lax
from jax.experimental import pallas as pl
from jax.experimental.pallas import tpu as pltpu

GRID_W = 64
EPS = 1e-6
FFN_STEP = 0.5
LRU_BLOCKS = 8
LRU_C = 8.0
CONV_W = 4
CONV_LEFT = 2
RET_HEADS = 4
RET_CHUNK = 128
RET_THETA = 10000.0
POOL_WINDOWS = (2, 4, 8, 16)
POOL_GROUP = 128
ATT_HEADS = 12
KV_HEADS = 4
GROUP = ATT_HEADS // KV_HEADS
HEAD_DIM = 128
ROPE_THETA = 10000.0

VMEM_LIMIT_BYTES = 56 * 1024 * 1024
SUBLANES = 8

ROW_TILE = 512
PROJ_ROW_TILE = 1024
OUT_ROW_TILE = 256
FFN_FF_TILE = 512
PROJ_COL_TILE = 1536
SEQ_BLOCK = 256
ATT_Q_TILE = 512
ATT_KV_TILE = 1408
POOL_BLOCK = 512
LOG2_E = 1.4426950408889634
NORM_ROWS = 16
NORM_UNROLL = 16

BF16 = jnp.bfloat16
F32 = jnp.float32


def _params(*sem):
    return pltpu.CompilerParams(dimension_semantics=sem, vmem_limit_bytes=VMEM_LIMIT_BYTES)


def _dot(a, b):
    return jnp.dot(a, b, preferred_element_type=F32)


def _sigmoid(x):
    return 0.5 * jnp.tanh(0.5 * x) + 0.5


def _modulate_rows(x_ref, gpre_ref, mod_ref, row_s, h_ref):
    _set_modulate_rows(gpre_ref, mod_ref, row_s)

    def body(i, carry):
        _modulate_group(x_ref, row_s, h_ref, i * NORM_ROWS)
        return carry

    lax.fori_loop(0, x_ref.shape[0] // NORM_ROWS, body, 0, unroll=NORM_UNROLL)


def _set_modulate_rows(gpre_ref, mod_ref, row_s):
    d = gpre_ref.shape[1]
    row_s[0] = jnp.broadcast_to(gpre_ref[...] * (1.0 + mod_ref[0, 1:2, :]), (SUBLANES, d))
    row_s[1] = jnp.broadcast_to(mod_ref[0, 0:1, :], (SUBLANES, d))


def _modulate_group(x_ref, row_s, h_ref, start):
    d = x_ref.shape[1]
    r = pl.multiple_of(start, NORM_ROWS)
    x = x_ref[pl.ds(r, NORM_ROWS), :]
    ms = jnp.mean(x * x, axis=-1, keepdims=True)
    xn = (x * lax.rsqrt(ms + EPS)).reshape(NORM_ROWS // SUBLANES, SUBLANES, d)
    h = xn * row_s[0] + row_s[1]
    h_ref[pl.ds(r, NORM_ROWS), :] = h.reshape(NORM_ROWS, d).astype(h_ref.dtype)


def _gated_residual_rows(x_ref, y_ref, gpost_ref, mod_ref, row_s, o_ref, step):
    _set_residual_row(gpost_ref, mod_ref, row_s, step)

    def body(i, carry):
        _residual_group(x_ref, y_ref, row_s, o_ref, i * NORM_ROWS)
        return carry

    lax.fori_loop(0, x_ref.shape[0] // NORM_ROWS, body, 0, unroll=NORM_UNROLL)


def _set_residual_row(gpost_ref, mod_ref, row_s, step):
    d = gpost_ref.shape[1]
    row_s[2] = jnp.broadcast_to((step * mod_ref[0, 2:3, :]) * gpost_ref[...], (SUBLANES, d))


def _residual_group(x_ref, y_ref, row_s, o_ref, start):
    d = x_ref.shape[1]
    r = pl.multiple_of(start, NORM_ROWS)
    y = y_ref[pl.ds(r, NORM_ROWS), :]
    ms = jnp.mean(y * y, axis=-1, keepdims=True)
    yn = (y * lax.rsqrt(ms + EPS)).reshape(NORM_ROWS // SUBLANES, SUBLANES, d)
    o_ref[pl.ds(r, NORM_ROWS), :] = x_ref[pl.ds(r, NORM_ROWS), :] + (yn * row_s[2]).reshape(NORM_ROWS, d)


def _mod_spec(d, tm, seq_len, n_groups, tile_offset=0):
    return pl.BlockSpec(
        (1, 3, d), lambda i, *_: (jnp.minimum(((i + tile_offset) * tm) // seq_len, n_groups - 1), 0, 0))


def _mod_kernel(c_ref, w_ref, b_ref, o_ref):
    c = c_ref[...]
    sc = c * jax.nn.sigmoid(c)
    o_ref[0] = _dot(sc.astype(BF16), w_ref[0].astype(BF16)) + b_ref[0]


def modulation_vectors(c_all, mod_w, mod_b):
    depth, d, n = mod_w.shape
    tn = PROJ_COL_TILE
    return pl.pallas_call(
        _mod_kernel,
        grid=(depth, n // tn),
        in_specs=[
            pl.BlockSpec((SUBLANES, d), lambda l, j: (0, 0)),
            pl.BlockSpec((1, d, tn), lambda l, j: (l, 0, j)),
            pl.BlockSpec((1, 1, tn), lambda l, j: (l, 0, j)),
        ],
        out_specs=pl.BlockSpec((1, SUBLANES, tn), lambda l, j: (l, 0, j)),
        out_shape=jax.ShapeDtypeStruct((depth, SUBLANES, n), F32),
        compiler_params=_params("arbitrary", "arbitrary"),
        name="modulation_vectors",
    )(c_all, mod_w, mod_b.reshape(depth, 1, n))


def _ffn_kernel(x_ref, mod_ref, gpre_ref, gpost_ref, wg_ref, wu_ref, wd_ref, o_ref, h_ref, acc_ref, row_s):
    j = pl.program_id(1)

    @pl.when(j == 0)
    def _():
        _modulate_rows(x_ref, gpre_ref, mod_ref, row_s, h_ref)
        acc_ref[...] = jnp.zeros_like(acc_ref)

    h = h_ref[...]
    g = _dot(h, wg_ref[...])
    u = _dot(h, wu_ref[...])
    a = (g * jax.nn.sigmoid(g)) * u
    acc_ref[...] += _dot(a.astype(BF16), wd_ref[...])

    @pl.when(j == pl.num_programs(1) - 1)
    def _():
        _gated_residual_rows(x_ref, acc_ref, gpost_ref, mod_ref, row_s, o_ref, FFN_STEP)


def _ffn_kernel_into(x_ref, mod_ref, gpre_ref, gpost_ref, wg_ref, wu_ref, wd_ref, into_ref, *rest):
    del into_ref
    _ffn_kernel(x_ref, mod_ref, gpre_ref, gpost_ref, wg_ref, wu_ref, wd_ref, *rest)


def ffn_sublayer(x, n_rows, mod, g_pre, g_post, w_gate, w_up, w_down, which, seq_len,
                 out_rows=None, tile_offset=0, into=None):
    d = x.shape[1]
    f = w_gate.shape[-1]
    tm, tf = ROW_TILE, FFN_FF_TILE
    li, fi = which
    out_rows = n_rows if out_rows is None else out_rows
    assert n_rows % tm == 0 and f % tf == 0 and seq_len % tm == 0 and out_rows >= tile_offset * tm + n_rows
    in_specs = [
        pl.BlockSpec((tm, d), lambda i, j: (i, 0)),
        _mod_spec(d, tm, seq_len, mod.shape[0], tile_offset),
        pl.BlockSpec((1, d), lambda i, j: (0, 0)),
        pl.BlockSpec((1, d), lambda i, j: (0, 0)),
        pl.BlockSpec((None, None, d, tf), lambda i, j: (li, fi, 0, j)),
        pl.BlockSpec((None, None, d, tf), lambda i, j: (li, fi, 0, j)),
        pl.BlockSpec((None, None, tf, d), lambda i, j: (li, fi, j, 0)),
    ]
    args = [x, mod, g_pre.reshape(1, d), g_post.reshape(1, d), w_gate, w_up, w_down]
    aliases = {}
    if into is not None:
        assert into.shape == (out_rows, d)
        in_specs.append(pl.BlockSpec(memory_space=pl.ANY))
        aliases = {len(args): 0}
        args.append(into)
    return pl.pallas_call(
        _ffn_kernel if into is None else _ffn_kernel_into,
        grid=(n_rows // tm, f // tf),
        in_specs=in_specs,
        out_specs=pl.BlockSpec((tm, d), lambda i, j: (i + tile_offset, 0)),
        out_shape=jax.ShapeDtypeStruct((out_rows, d), F32),
        input_output_aliases=aliases,
        scratch_shapes=[pltpu.VMEM((tm, d), BF16), pltpu.VMEM((tm, d), F32), pltpu.VMEM((3, SUBLANES, d), F32)],
        compiler_params=_params("arbitrary", "arbitrary"),
        name="ffn_sublayer",
    )(*args)


def _in_proj_kernel(x_ref, mod_ref, gpre_ref, w_ref, o_ref, h_ref, row_s):
    @pl.when(pl.program_id(1) == 0)
    def _():
        _modulate_rows(x_ref, gpre_ref, mod_ref, row_s, h_ref)

    o_ref[...] = _dot(h_ref[...], w_ref[...])


def mixer_in_proj(x, mod, g_pre, w_in, seq_len):
    rows, d = x.shape
    n = w_in.shape[1]
    tm, tn = PROJ_ROW_TILE, PROJ_COL_TILE
    assert rows % tm == 0 and n % tn == 0 and seq_len % tm == 0
    return pl.pallas_call(
        _in_proj_kernel,
        grid=(rows // tm, n // tn),
        in_specs=[
            pl.BlockSpec((tm, d), lambda i, j: (i, 0)),
            _mod_spec(d, tm, seq_len, mod.shape[0]),
            pl.BlockSpec((1, d), lambda i, j: (0, 0)),
            pl.BlockSpec((d, tn), lambda i, j: (0, j)),
        ],
        out_specs=pl.BlockSpec((tm, tn), lambda i, j: (i, j)),
        out_shape=jax.ShapeDtypeStruct((rows, n), F32),
        scratch_shapes=[pltpu.VMEM((tm, d), BF16), pltpu.VMEM((3, SUBLANES, d), F32)],
        compiler_params=_params("arbitrary", "arbitrary"),
        name="mixer_in_proj",
    )(x, mod, g_pre.reshape(1, d), w_in)


def _lru_kernel(*refs, reverse, n_lat, final):
    if final:
        (r_ref, rp_ref, rn_ref, cw_ref, cb_ref, wa_ref, ba_ref, wx_ref, bx_ref, lam_ref,
         hf_ref, g_ref, o_ref, a_s, b_s, h_s, hs_s) = refs
    else:
        (r_ref, rp_ref, rn_ref, cw_ref, cb_ref, wa_ref, ba_ref, wx_ref, bx_ref, lam_ref,
         o_ref, a_s, b_s, h_s) = refs
        hs_s = o_ref
    s = pl.program_id(1)
    tb, w = r_ref.shape
    is_lat = s > 0
    k = (n_lat - s) if reverse else (s - 1)
    has_prev = jnp.logical_and(is_lat, k > 0)
    has_next = jnp.logical_and(is_lat, k < n_lat - 1)

    prev = jnp.where(has_prev, rp_ref[...], 0.0)
    nxt = jnp.where(has_next, rn_ref[...], 0.0)
    ext = jnp.concatenate([prev, r_ref[...], nxt], axis=0)
    base = SUBLANES - CONV_LEFT
    u = ext[base:base + tb] * cw_ref[0:1, :]
    for t in range(1, CONV_W):
        u = u + ext[base + t:base + t + tb] * cw_ref[t:t + 1, :]
    u = u + cb_ref[...]

    ub = u.astype(BF16)
    blk = w // LRU_BLOCKS
    ra = jnp.concatenate([_dot(ub[:, n * blk:(n + 1) * blk], wa_ref[n]) for n in range(LRU_BLOCKS)], axis=1)
    xa = jnp.concatenate([_dot(ub[:, n * blk:(n + 1) * blk], wx_ref[n]) for n in range(LRU_BLOCKS)], axis=1)
    r_gate = _sigmoid(ra + ba_ref[...])
    i_gate = _sigmoid(xa + bx_ref[...])
    lam = lam_ref[...]
    softplus_neg_lam = jnp.maximum(-lam, 0.0) + jnp.log1p(jnp.exp(-jnp.abs(lam)))
    log_a = -LRU_C * r_gate * softplus_neg_lam
    a = jnp.exp(log_a)
    one_minus_a2 = -jnp.tanh(log_a) * (a * a + 1.0)
    a_s[...] = a
    b_s[...] = jnp.sqrt(one_minus_a2) * (i_gate * u)

    @pl.when(s == 0)
    def _():
        h_s[...] = jnp.zeros_like(h_s)

    def body(t, h):
        tt = (tb - 1 - t) if reverse else t
        h = a_s[pl.ds(tt, 1), :] * h + b_s[pl.ds(tt, 1), :]
        hs_s[pl.ds(tt, 1), :] = h
        return h

    h_s[...] = lax.fori_loop(0, tb, body, h_s[...], unroll=8)

    if final:
        o_ref[...] = (jax.nn.gelu(g_ref[...]) * (hf_ref[...] + hs_s[...])).astype(o_ref.dtype)


def rglru_direction(proj, r_col, g_col, conv_w, conv_b, wa, ba, wx, bx, lam, h_fwd, *, batch, seq_len, ctx_len):
    rows = proj.shape[0]
    w = wa.shape[0] * wa.shape[1]
    tb = SEQ_BLOCK
    assert ctx_len == tb and seq_len % tb == 0
    n_lat = seq_len // tb
    ctx0 = batch * n_lat
    hb = tb // SUBLANES
    n_halo = rows // SUBLANES
    final = h_fwd is not None

    def blk(b, s):
        k = (n_lat - s) if final else (s - 1)
        return jnp.where(s == 0, ctx0 + b, b * n_lat + k)

    row_spec = lambda col: pl.BlockSpec((tb, w), lambda b, s: (blk(b, s), col))
    vec = lambda n: pl.BlockSpec((n, w), lambda b, s: (0, 0))
    mat = pl.BlockSpec(wa.shape, lambda b, s: (0, 0, 0))
    in_specs = [
        row_spec(r_col),
        pl.BlockSpec((SUBLANES, w), lambda b, s: (jnp.maximum(blk(b, s) * hb - 1, 0), r_col)),
        pl.BlockSpec((SUBLANES, w), lambda b, s: (jnp.minimum((blk(b, s) + 1) * hb, n_halo - 1), r_col)),
        vec(CONV_W), vec(1), mat, vec(1), mat, vec(1), vec(1),
    ]
    args = [proj, proj, proj, conv_w, conv_b.reshape(1, w), wa, ba.reshape(1, w), wx, bx.reshape(1, w),
            lam.reshape(1, w)]
    scratch = [pltpu.VMEM((tb, w), F32), pltpu.VMEM((tb, w), F32), pltpu.VMEM((1, w), F32)]
    if final:
        in_specs += [row_spec(0), row_spec(g_col)]
        args += [h_fwd, proj]
        scratch.append(pltpu.VMEM((tb, w), F32))
    return pl.pallas_call(
        functools.partial(_lru_kernel, reverse=final, n_lat=n_lat, final=final),
        grid=(batch, n_lat + 1),
        in_specs=in_specs,
        out_specs=row_spec(0),
        out_shape=jax.ShapeDtypeStruct((rows, w), BF16 if final else F32),
        scratch_shapes=scratch,
        compiler_params=_params("arbitrary", "arbitrary"),
        name="rglru_reverse" if final else "rglru_forward",
    )(*args)


def _rotate_halves(x, cos, sin):
    half = x.shape[-1] // 2
    x1, x2 = x[:, :half], x[:, half:]
    return jnp.concatenate([x1 * cos - x2 * sin, x1 * sin + x2 * cos], axis=-1)


def _retention_kernel(lg_ref, qf_ref, kf_ref, vf_ref, cf_ref, sf_ref, qb_ref, kb_ref, vb_ref, cb_ref, sb_ref,
                      of_ref, ob_ref, state, *, n_ctx):
    s = pl.program_id(1)
    c, width = qf_ref.shape
    dk = width // RET_HEADS
    is_lat = s >= n_ctx

    @pl.when(s == 0)
    def _():
        state[...] = jnp.zeros_like(state)

    ii = lax.broadcasted_iota(jnp.int32, (c, 1), 0).astype(F32)
    diff = ii - lax.broadcasted_iota(jnp.int32, (1, c), 1).astype(F32)
    k_scale = dk ** -0.5
    dirs = ((qf_ref, kf_ref, vf_ref, cf_ref, sf_ref, of_ref), (qb_ref, kb_ref, vb_ref, cb_ref, sb_ref, ob_ref))
    for d, (q_ref, k_ref, v_ref, cos_ref, sin_ref, o_ref) in enumerate(dirs):
        cos = jnp.where(is_lat, cos_ref[...], 1.0)
        sin = jnp.where(is_lat, sin_ref[...], 0.0)
        for h in range(RET_HEADS):
            lg = lg_ref[d, h]
            sl = slice(h * dk, (h + 1) * dk)
            q = _rotate_halves(q_ref[:, sl], cos, sin)
            k = _rotate_halves(k_ref[:, sl], cos, sin) * k_scale
            vb = v_ref[:, sl].astype(BF16)
            if d == 0:
                intra = jnp.where(diff >= 0, jnp.exp(lg * jnp.maximum(diff, 0.0)), 0.0)
                q_dec = jnp.exp(lg * (ii + 1.0))
                k_dec = jnp.exp(lg * (c - 1.0 - ii))
            else:
                intra = jnp.where(diff <= 0, jnp.exp(lg * jnp.maximum(-diff, 0.0)), 0.0)
                q_dec = jnp.exp(lg * (c - ii))
                k_dec = jnp.exp(lg * ii)
            s_dec = jnp.exp(lg * jnp.full((1, 1), float(c), F32))
            qb = q.astype(BF16)
            scores = lax.dot_general(qb, k.astype(BF16), (((1,), (1,)), ((), ())), preferred_element_type=F32)
            scores = scores * intra
            st = state[d, h]
            o = _dot(scores.astype(BF16), vb) + _dot((q * q_dec).astype(BF16), st.astype(BF16))
            o_ref[:, sl] = o
            kd_t = (k * k_dec).T.astype(BF16)
            state[d, h] = st * s_dec + _dot(kd_t, vb)


def retention_bidir(proj, q_col, k_col, v_col, log_g, cos, sin, *, batch, seq_len, ctx_len):
    rows = proj.shape[0]
    c = RET_CHUNK
    width = RET_HEADS * 2 * cos.shape[1]
    dk = width // RET_HEADS
    n_lat, n_ctx = seq_len // c, ctx_len // c
    ctx0 = batch * n_lat
    last = n_lat + n_ctx - 1

    def fwd(b, s):
        return jnp.where(s < n_ctx, ctx0 + b * n_ctx + s, b * n_lat + s - n_ctx)

    def bwd(b, s):
        return jnp.where(s < n_ctx, ctx0 + b * n_ctx + (n_ctx - 1 - s), b * n_lat + (last - s))

    def specs(idx):
        tab = pl.BlockSpec((c, dk // 2), lambda b, s: (jnp.where(s >= n_ctx, idx(b, s) - b * n_lat, 0), 0))
        return [pl.BlockSpec((c, width), lambda b, s, col=col: (idx(b, s), col)) for col in (q_col, k_col, v_col)] + [tab, tab]

    out_spec = lambda idx: pl.BlockSpec((c, width), lambda b, s: (idx(b, s), 0))
    return pl.pallas_call(
        functools.partial(_retention_kernel, n_ctx=n_ctx),
        grid=(batch, n_lat + n_ctx),
        in_specs=[pl.BlockSpec(memory_space=pltpu.SMEM)] + specs(fwd) + specs(bwd),
        out_specs=[out_spec(fwd), out_spec(bwd)],
        out_shape=[jax.ShapeDtypeStruct((rows, width), F32)] * 2,
        scratch_shapes=[pltpu.VMEM((2, RET_HEADS, dk, dk), F32)],
        compiler_params=_params("arbitrary", "arbitrary"),
        name="retention_bidir",
    )(log_g, proj, proj, proj, cos, sin, proj, proj, proj, cos, sin)


def _even_out_kernel(x_ref, mod_ref, gpost_ref, lru_ref, of_ref, ob_ref, gate_ref, gn_ref, w_ref, o_ref, y_s, row_s):
    o = of_ref[...] + ob_ref[...]
    width = o.shape[1]
    dv = width // RET_HEADS
    parts = []
    for h in range(RET_HEADS):
        oh = o[:, h * dv:(h + 1) * dv]
        mu = jnp.mean(oh, axis=-1, keepdims=True)
        cen = oh - mu
        var = jnp.mean(cen * cen, axis=-1, keepdims=True)
        parts.append(cen * lax.rsqrt(var + EPS))
    gate = gate_ref[...]
    ret = jnp.concatenate(parts, axis=-1) * gn_ref[...] * (gate * jax.nn.sigmoid(gate))
    lw = lru_ref.shape[1]
    y_s[...] = _dot(lru_ref[...], w_ref[0:lw, :]) + _dot(ret.astype(BF16), w_ref[lw:lw + width, :])
    _gated_residual_rows(x_ref, y_s, gpost_ref, mod_ref, row_s, o_ref, 1.0)


def even_mixer_out(x, mod, g_post, lru, o_fwd, o_bwd, proj, gate_col, gn_g, w_out, seq_len):
    rows, d = x.shape
    tm = OUT_ROW_TILE
    lw, rw = lru.shape[1], o_fwd.shape[1]
    return pl.pallas_call(
        _even_out_kernel,
        grid=(rows // tm,),
        in_specs=[
            pl.BlockSpec((tm, d), lambda i: (i, 0)),
            _mod_spec(d, tm, seq_len, mod.shape[0]),
            pl.BlockSpec((1, d), lambda i: (0, 0)),
            pl.BlockSpec((tm, lw), lambda i: (i, 0)),
            pl.BlockSpec((tm, rw), lambda i: (i, 0)),
            pl.BlockSpec((tm, rw), lambda i: (i, 0)),
            pl.BlockSpec((tm, rw), lambda i: (i, gate_col)),
            pl.BlockSpec((1, rw), lambda i: (0, 0)),
            pl.BlockSpec((lw + rw, d), lambda i: (0, 0)),
        ],
        out_specs=pl.BlockSpec((tm, d), lambda i: (i, 0)),
        out_shape=jax.ShapeDtypeStruct((rows, d), F32),
        scratch_shapes=[pltpu.VMEM((tm, d), F32), pltpu.VMEM((3, SUBLANES, d), F32)],
        compiler_params=_params("arbitrary"),
        name="even_mixer_out",
    )(x, mod, g_post.reshape(1, d), lru, o_fwd, o_bwd, proj, gn_g.reshape(1, rw), w_out)


def _head_norm_rotary(x, g, cos_full, sin_signed):
    ms = jnp.mean(x * x, axis=-1, keepdims=True)
    y = x * lax.rsqrt(ms + EPS) * g
    return y * cos_full + pltpu.roll(y, HEAD_DIM // 2, axis=1) * sin_signed


def _kv_prep_kernel(k_ref, v_ref, kg_ref, cos_ref, sin_ref, ko_ref, vo_ref):
    is_lat = pl.program_id(1) > 0
    cos = jnp.where(is_lat, cos_ref[...], 1.0)
    sin = jnp.where(is_lat, sin_ref[...], 0.0)
    for h in range(KV_HEADS):
        sl = slice(h * HEAD_DIM, (h + 1) * HEAD_DIM)
        ko_ref[:, sl] = _head_norm_rotary(k_ref[:, sl], kg_ref[...], cos, sin).astype(BF16)
        vo_ref[sl, :] = v_ref[:, sl].T.astype(BF16)


def kv_prepare(proj, k_col, v_col, k_g, cos_full, sin_signed, *, batch, seq_len, ctx_len):
    tb = SEQ_BLOCK
    kvw = KV_HEADS * HEAD_DIM
    assert ctx_len == tb
    n_lat = seq_len // tb
    ctx0 = batch * n_lat
    n_keys = seq_len + ctx_len
    src = lambda col: pl.BlockSpec((tb, kvw), lambda b, s: (jnp.where(s == 0, ctx0 + b, b * n_lat + s - 1), col))
    tab = pl.BlockSpec((tb, HEAD_DIM), lambda b, s: (jnp.maximum(s - 1, 0), 0))
    return pl.pallas_call(
        _kv_prep_kernel,
        grid=(batch, n_lat + 1),
        in_specs=[src(k_col), src(v_col), pl.BlockSpec((1, HEAD_DIM), lambda b, s: (0, 0)), tab, tab],
        out_specs=[pl.BlockSpec((tb, kvw), lambda b, s: (b * (n_lat + 1) + s, 0)),
                   pl.BlockSpec((kvw, tb), lambda b, s: (b, s))],
        out_shape=[jax.ShapeDtypeStruct((batch * n_keys, kvw), BF16),
                   jax.ShapeDtypeStruct((batch * kvw, n_keys), BF16)],
        compiler_params=_params("arbitrary", "arbitrary"),
        name="kv_prepare",
    )(proj, proj, k_g.reshape(1, HEAD_DIM), cos_full, sin_signed)


def _attention_kernel(q0_ref, q1_ref, q2_ref, qg_ref, cos_ref, sin_ref, k_ref, vt_ref, o_ref, s_ref):
    tq = q0_ref.shape[0]
    n_keys = k_ref.shape[0]
    tk = ATT_KV_TILE
    cos, sin = cos_ref[...], sin_ref[...]
    c = HEAD_DIM ** -0.5 * LOG2_E
    q_t = [(_head_norm_rotary(r[...], qg_ref[...], cos, sin) * c).T.astype(BF16) for r in (q0_ref, q1_ref, q2_ref)]

    n_chunks = n_keys // tk

    def scores(j, g):
        start = pl.multiple_of(j * tk, tk)
        return _dot(k_ref[pl.ds(start, tk), :], q_t[g])

    def update(j, s, state):
        m, l, acc = state
        start = pl.multiple_of(j * tk, tk)
        m_new = jnp.maximum(m, jnp.max(s, axis=0, keepdims=True))
        alpha = jnp.exp2(m - m_new)
        p = jnp.exp2(s - m_new)
        l = alpha * l + jnp.sum(p, axis=0, keepdims=True)
        acc = alpha * acc + _dot(vt_ref[:, pl.ds(start, tk)], p.astype(BF16))
        return m_new, l, acc

    s_ref[...] = scores(0, 0)

    def body(j, carry):
        state = list(carry)
        s_cur = s_ref[...]
        for g in range(GROUP):
            s_next = scores(j, g + 1) if g + 1 < GROUP else scores(jnp.minimum(j + 1, n_chunks - 1), 0)
            state[g] = update(j, s_cur, state[g])
            s_cur = s_next
        s_ref[...] = s_cur
        return tuple(state)

    init = tuple((jnp.full((1, tq), -jnp.inf, F32), jnp.zeros((1, tq), F32), jnp.zeros((HEAD_DIM, tq), F32))
                 for _ in range(GROUP))
    final = lax.fori_loop(0, n_chunks, body, init, unroll=True)
    for g, (_, l, acc) in enumerate(final):
        o_ref[:, g * HEAD_DIM:(g + 1) * HEAD_DIM] = (acc / l).T.astype(o_ref.dtype)


def gqa_attention(proj, q_col0, q_g, cos_full, sin_signed, keys, values_t, *, batch, seq_len, ctx_len):
    tq = ATT_Q_TILE
    n_keys = seq_len + ctx_len
    assert seq_len % tq == 0 and n_keys % ATT_KV_TILE == 0
    nq = seq_len // tq
    qspec = lambda g: pl.BlockSpec((tq, HEAD_DIM), lambda b, kh, i: (b * nq + i, q_col0 + kh * GROUP + g))
    tab = pl.BlockSpec((tq, HEAD_DIM), lambda b, kh, i: (i, 0))
    return pl.pallas_call(
        _attention_kernel,
        grid=(batch, KV_HEADS, nq),
        in_specs=[qspec(0), qspec(1), qspec(2), pl.BlockSpec((1, HEAD_DIM), lambda b, kh, i: (0, 0)), tab, tab,
                  pl.BlockSpec((n_keys, HEAD_DIM), lambda b, kh, i: (b, kh)),
                  pl.BlockSpec((HEAD_DIM, n_keys), lambda b, kh, i: (b * KV_HEADS + kh, 0))],
        out_specs=pl.BlockSpec((tq, GROUP * HEAD_DIM), lambda b, kh, i: (b * nq + i, kh)),
        out_shape=jax.ShapeDtypeStruct((batch * seq_len, ATT_HEADS * HEAD_DIM), BF16),
        scratch_shapes=[pltpu.VMEM((ATT_KV_TILE, tq), F32)],
        compiler_params=_params("arbitrary", "arbitrary", "arbitrary"),
        name="gqa_attention",
    )(proj, proj, proj, q_g.reshape(1, HEAD_DIM), cos_full, sin_signed, keys, values_t)


def _pool_kernel(x_ref, xp_ref, xn_ref, w_ref, scale_ref, o_ref, *, seq_len):
    i = pl.program_id(1)
    tb = x_ref.shape[0]
    x = x_ref[...]
    prev = jnp.where(i > 0, xp_ref[...], 0.0)
    nxt = jnp.where(i < pl.num_programs(1) - 1, xn_ref[...], 0.0)
    ext = jnp.concatenate([prev, x, nxt], axis=0)
    t = i * tb + lax.broadcasted_iota(jnp.int32, (tb, 1), 0)
    for gi, win in enumerate(POOL_WINDOWS):
        sl = slice(gi * POOL_GROUP, (gi + 1) * POOL_GROUP)
        half = win // 2
        e = ext[:, sl]
        tot = e[SUBLANES - half:SUBLANES - half + tb]
        for off in range(1 - half, half):
            tot = tot + e[SUBLANES + off:SUBLANES + off + tb]
        cnt = (jnp.minimum(t + half, seq_len) - jnp.maximum(t - half, 0)).astype(F32)
        centred = tot / cnt - x[:, sl]
        o_ref[:, sl] = (_dot(centred.astype(BF16), w_ref[gi]) * scale_ref[:, sl]).astype(o_ref.dtype)


def multiscale_pool(proj, pool_w, pool_scale, *, batch, seq_len):
    tb = POOL_BLOCK
    pw = POOL_GROUP * len(POOL_WINDOWS)
    nb = seq_len // tb
    hb = tb // SUBLANES
    assert max(POOL_WINDOWS) // 2 <= SUBLANES
    return pl.pallas_call(
        functools.partial(_pool_kernel, seq_len=seq_len),
        grid=(batch, nb),
        in_specs=[
            pl.BlockSpec((tb, pw), lambda b, i: (b * nb + i, 0)),
            pl.BlockSpec((SUBLANES, pw), lambda b, i: (jnp.maximum((b * nb + i) * hb - 1, 0), 0)),
            pl.BlockSpec((SUBLANES, pw), lambda b, i: ((b * nb + i + 1) * hb, 0)),
            pl.BlockSpec(pool_w.shape, lambda b, i: (0, 0, 0)),
            pl.BlockSpec((1, pw), lambda b, i: (0, 0)),
        ],
        out_specs=pl.BlockSpec((tb, pw), lambda b, i: (b * nb + i, 0)),
        out_shape=jax.ShapeDtypeStruct((batch * seq_len, pw), BF16),
        compiler_params=_params("arbitrary", "arbitrary"),
        name="multiscale_pool",
    )(proj, proj, proj, pool_w, pool_scale.reshape(1, pw))


def _odd_out_kernel(x_ref, mod_ref, gpost_ref, pool_ref, att_ref, w_ref, o_ref, y_s, row_s):
    pw = pool_ref.shape[1]
    y_s[...] = _dot(pool_ref[...], w_ref[0:pw, :]) + _dot(att_ref[...], w_ref[pw:, :])
    _gated_residual_rows(x_ref, y_s, gpost_ref, mod_ref, row_s, o_ref, 1.0)


def odd_mixer_out(x, n_rows, mod, g_post, pooled, att, w_out, seq_len):
    d = x.shape[1]
    tm = OUT_ROW_TILE
    pw, aw = pooled.shape[1], att.shape[1]
    return pl.pallas_call(
        _odd_out_kernel,
        grid=(n_rows // tm,),
        in_specs=[
            pl.BlockSpec((tm, d), lambda i: (i, 0)),
            _mod_spec(d, tm, seq_len, mod.shape[0]),
            pl.BlockSpec((1, d), lambda i: (0, 0)),
            pl.BlockSpec((tm, pw), lambda i: (i, 0)),
            pl.BlockSpec((tm, aw), lambda i: (i, 0)),
            pl.BlockSpec((pw + aw, d), lambda i: (0, 0)),
        ],
        out_specs=pl.BlockSpec((tm, d), lambda i: (i, 0)),
        out_shape=jax.ShapeDtypeStruct((n_rows, d), F32),
        scratch_shapes=[pltpu.VMEM((tm, d), F32), pltpu.VMEM((3, SUBLANES, d), F32)],
        compiler_params=_params("arbitrary"),
        name="odd_mixer_out",
    )(x, mod, g_post.reshape(1, d), pooled, att, w_out)


def kernel(x, c, ctx, c_ctx, mod_w, mod_b, norm_pre, norm_post, ffn_gate, ffn_up, ffn_down,
           ev_w_in, ev_w_out, lru_conv_w, lru_conv_b, lru_wa, lru_ba, lru_wx, lru_bx, lru_lambda,
           ret_decay_logit, ret_gn, od_w_in, od_w_out, pool_w, pool_scale, q_norm, k_norm):
    B, S, D = x.shape
    Lc = ctx.shape[1]
    depth = mod_w.shape[0]
    assert depth == 2 and B < SUBLANES
    geom = dict(batch=B, seq_len=S, ctx_len=Lc)
    n_lat_rows = B * S

    grid_rows = S // GRID_W
    row = jnp.repeat(jnp.arange(grid_rows, dtype=F32), GRID_W)
    col = jnp.tile(jnp.arange(GRID_W, dtype=F32), grid_rows)
    n_ax = HEAD_DIM // 4
    f_ax = ROPE_THETA ** (-jnp.arange(n_ax, dtype=F32) / n_ax)
    ang2 = jnp.concatenate([row[:, None] * f_ax, col[:, None] * f_ax], axis=-1)
    cos2, sin2 = jnp.cos(ang2), jnp.sin(ang2)
    cos_full = jnp.concatenate([cos2, cos2], axis=-1)
    sin_signed = jnp.concatenate([-sin2, sin2], axis=-1)
    ret_dk = ev_w_out.shape[1] // 2 // RET_HEADS
    n_r = ret_dk // 2
    f_r = RET_THETA ** (-jnp.arange(n_r, dtype=F32) / n_r)
    ang1 = jnp.arange(S, dtype=F32)[:, None] * f_r
    cos1, sin1 = jnp.cos(ang1), jnp.sin(ang1)

    wg, wu, wd = ffn_gate.astype(BF16), ffn_up.astype(BF16), ffn_down.astype(BF16)

    c_all = jnp.zeros((SUBLANES, D), F32).at[:B].set(c).at[B].set(c_ctx)
    mods = modulation_vectors(c_all, mod_w, mod_b).reshape(depth, SUBLANES, 3, 3, D)[:, :B + 1]

    n_ctx_rows = B * Lc
    n_rows = n_lat_rows + n_ctx_rows

    li, e = 0, 0
    mod = lambda sub: mods[li, :, sub]

    def ffn(rows_in, n, sub, fi, **kw):
        return ffn_sublayer(rows_in, n, mod(sub), norm_pre[li, sub], norm_post[li, sub], wg, wu, wd, (li, fi), S,
                            **kw)

    xs = ffn(x.reshape(n_lat_rows, D), n_lat_rows, 0, 0, out_rows=n_rows)
    xs = ffn(ctx.reshape(n_ctx_rows, D), n_ctx_rows, 0, 0, out_rows=n_rows, tile_offset=n_lat_rows // ROW_TILE,
             into=xs)
    proj = mixer_in_proj(xs, mod(1), norm_pre[li, 1], ev_w_in[e].astype(BF16), S)
    lru_args = lambda d: (lru_conv_w[e], lru_conv_b[e], lru_wa[e, d].astype(BF16), lru_ba[e, d],
                          lru_wx[e, d].astype(BF16), lru_bx[e, d], lru_lambda[e, d])
    h_fwd = rglru_direction(proj, 1, 0, *lru_args(0), None, **geom)
    lru = rglru_direction(proj, 1, 0, *lru_args(1), h_fwd, **geom)
    log_g = -jax.nn.softplus(-ret_decay_logit[e].astype(F32))
    o_fwd, o_bwd = retention_bidir(proj, 2, 3, 4, log_g, cos1, sin1, **geom)
    xs = even_mixer_out(xs, mod(1), norm_post[li, 1], lru, o_fwd, o_bwd, proj, 5, ret_gn[e],
                        ev_w_out[e].astype(BF16), S)
    xs = ffn(xs, n_rows, 2, 1)

    li, o = 1, 0
    xs = ffn(xs, n_rows, 0, 0)
    proj = mixer_in_proj(xs, mod(1), norm_pre[li, 1], od_w_in[o].astype(BF16), S)
    pw = POOL_GROUP * len(POOL_WINDOWS)
    kvw = KV_HEADS * HEAD_DIM
    q0 = pw // HEAD_DIM
    k_col = (pw + ATT_HEADS * HEAD_DIM) // kvw
    keys, values_t = kv_prepare(proj, k_col, k_col + 1, k_norm[o], cos_full, sin_signed, **geom)
    att = gqa_attention(proj, q0, q_norm[o], cos_full, sin_signed, keys, values_t, **geom)
    pooled = multiscale_pool(proj, pool_w[o].astype(BF16), pool_scale[o], batch=B, seq_len=S)
    xl = odd_mixer_out(xs, n_lat_rows, mod(1), norm_post[li, 1], pooled, att, od_w_out[o].astype(BF16), S)
    xl = ffn(xl, n_lat_rows, 2, 1)
    return xl.reshape(B, S, D)
```

```python
import functools

import jax
import jax.numpy as jnp
from jax import lax
from jax.experimental import pallas as pl
from jax.experimental.pallas import tpu as pltpu

GRID_W = 64
EPS = 1e-6
FFN_STEP = 0.5
LRU_BLOCKS = 8
LRU_C = 8.0
CONV_W = 4
CONV_LEFT = 2
RET_HEADS = 4
RET_CHUNK = 128
RET_THETA = 10000.0
POOL_WINDOWS = (2, 4, 8, 16)
POOL_GROUP = 128
ATT_HEADS = 12
KV_HEADS = 4
GROUP = ATT_HEADS // KV_HEADS
HEAD_DIM = 128
ROPE_THETA = 10000.0

VMEM_LIMIT_BYTES = 56 * 1024 * 1024
SUBLANES = 8

ROW_TILE = 512
PROJ_ROW_TILE = 1024
OUT_ROW_TILE = 256
FFN_FF_TILE = 512
PROJ_COL_TILE = 1536
SEQ_BLOCK = 256
ATT_Q_TILE = 512
ATT_KV_TILE = 1408
POOL_BLOCK = 512
LOG2_E = 1.4426950408889634
NORM_ROWS = 16
NORM_UNROLL = 16

BF16 = jnp.bfloat16
F32 = jnp.float32


def _params(*sem):
    return pltpu.CompilerParams(dimension_semantics=sem, vmem_limit_bytes=VMEM_LIMIT_BYTES)


def _dot(a, b):
    return jnp.dot(a, b, preferred_element_type=F32)


def _sigmoid(x):
    return 0.5 * jnp.tanh(0.5 * x) + 0.5


def _modulate_rows(x_ref, gpre_ref, mod_ref, row_s, h_ref):
    _set_modulate_rows(gpre_ref, mod_ref, row_s)

    def body(i, carry):
        _modulate_group(x_ref, row_s, h_ref, i * NORM_ROWS)
        return carry

    lax.fori_loop(0, x_ref.shape[0] // NORM_ROWS, body, 0, unroll=NORM_UNROLL)


def _set_modulate_rows(gpre_ref, mod_ref, row_s):
    d = gpre_ref.shape[1]
    row_s[0] = jnp.broadcast_to(gpre_ref[...] * (1.0 + mod_ref[0, 1:2, :]), (SUBLANES, d))
    row_s[1] = jnp.broadcast_to(mod_ref[0, 0:1, :], (SUBLANES, d))


def _modulate_group(x_ref, row_s, h_ref, start):
    d = x_ref.shape[1]
    r = pl.multiple_of(start, NORM_ROWS)
    x = x_ref[pl.ds(r, NORM_ROWS), :]
    ms = jnp.mean(x * x, axis=-1, keepdims=True)
    xn = (x * lax.rsqrt(ms + EPS)).reshape(NORM_ROWS // SUBLANES, SUBLANES, d)
    h = xn * row_s[0] + row_s[1]
    h_ref[pl.ds(r, NORM_ROWS), :] = h.reshape(NORM_ROWS, d).astype(h_ref.dtype)


def _gated_residual_rows(x_ref, y_ref, gpost_ref, mod_ref, row_s, o_ref, step):
    _set_residual_row(gpost_ref, mod_ref, row_s, step)

    def body(i, carry):
        _residual_group(x_ref, y_ref, row_s, o_ref, i * NORM_ROWS)
        return carry

    lax.fori_loop(0, x_ref.shape[0] // NORM_ROWS, body, 0, unroll=NORM_UNROLL)


def _set_residual_row(gpost_ref, mod_ref, row_s, step):
    d = gpost_ref.shape[1]
    row_s[2] = jnp.broadcast_to((step * mod_ref[0, 2:3, :]) * gpost_ref[...], (SUBLANES, d))


def _residual_group(x_ref, y_ref, row_s, o_ref, start):
    d = x_ref.shape[1]
    r = pl.multiple_of(start, NORM_ROWS)
    y = y_ref[pl.ds(r, NORM_ROWS), :]
    ms = jnp.mean(y * y, axis=-1, keepdims=True)
    yn = (y * lax.rsqrt(ms + EPS)).reshape(NORM_ROWS // SUBLANES, SUBLANES, d)
    o_ref[pl.ds(r, NORM_ROWS), :] = x_ref[pl.ds(r, NORM_ROWS), :] + (yn * row_s[2]).reshape(NORM_ROWS, d)


def _mod_spec(d, tm, seq_len, n_groups, tile_offset=0):
    return pl.BlockSpec(
        (1, 3, d), lambda i, *_: (jnp.minimum(((i + tile_offset) * tm) // seq_len, n_groups - 1), 0, 0))


def _mod_kernel(c_ref, w_ref, b_ref, o_ref):
    c = c_ref[...]
    sc = c * jax.nn.sigmoid(c)
    o_ref[0] = _dot(sc.astype(BF16), w_ref[0].astype(BF16)) + b_ref[0]


def modulation_vectors(c_all, mod_w, mod_b):
    depth, d, n = mod_w.shape
    tn = PROJ_COL_TILE
    return pl.pallas_call(
        _mod_kernel,
        grid=(depth, n // tn),
        in_specs=[
            pl.BlockSpec((SUBLANES, d), lambda l, j: (0, 0)),
            pl.BlockSpec((1, d, tn), lambda l, j: (l, 0, j)),
            pl.BlockSpec((1, 1, tn), lambda l, j: (l, 0, j)),
        ],
        out_specs=pl.BlockSpec((1, SUBLANES, tn), lambda l, j: (l, 0, j)),
        out_shape=jax.ShapeDtypeStruct((depth, SUBLANES, n), F32),
        compiler_params=_params("arbitrary", "arbitrary"),
        name="modulation_vectors",
    )(c_all, mod_w, mod_b.reshape(depth, 1, n))


def _ffn_kernel(x_ref, mod_ref, gpre_ref, gpost_ref, wg_ref, wu_ref, wd_ref, o_ref, h_ref, acc_ref, row_s):
    j = pl.program_id(1)

    @pl.when(j == 0)
    def _():
        _modulate_rows(x_ref, gpre_ref, mod_ref, row_s, h_ref)
        acc_ref[...] = jnp.zeros_like(acc_ref)

    h = h_ref[...]
    g = _dot(h, wg_ref[...])
    u = _dot(h, wu_ref[...])
    a = (g * jax.nn.sigmoid(g)) * u
    acc_ref[...] += _dot(a.astype(BF16), wd_ref[...])

    @pl.when(j == pl.num_programs(1) - 1)
    def _():
        _gated_residual_rows(x_ref, acc_ref, gpost_ref, mod_ref, row_s, o_ref, FFN_STEP)


def _ffn_kernel_into(x_ref, mod_ref, gpre_ref, gpost_ref, wg_ref, wu_ref, wd_ref, into_ref, *rest):
    del into_ref
    _ffn_kernel(x_ref, mod_ref, gpre_ref, gpost_ref, wg_ref, wu_ref, wd_ref, *rest)


def ffn_sublayer(x, n_rows, mod, g_pre, g_post, w_gate, w_up, w_down, which, seq_len,
                 out_rows=None, tile_offset=0, into=None):
    d = x.shape[1]
    f = w_gate.shape[-1]
    tm, tf = ROW_TILE, FFN_FF_TILE
    li, fi = which
    out_rows = n_rows if out_rows is None else out_rows
    assert n_rows % tm == 0 and f % tf == 0 and seq_len % tm == 0 and out_rows >= tile_offset * tm + n_rows
    in_specs = [
        pl.BlockSpec((tm, d), lambda i, j: (i, 0)),
        _mod_spec(d, tm, seq_len, mod.shape[0], tile_offset),
        pl.BlockSpec((1, d), lambda i, j: (0, 0)),
        pl.BlockSpec((1, d), lambda i, j: (0, 0)),
        pl.BlockSpec((None, None, d, tf), lambda i, j: (li, fi, 0, j)),
        pl.BlockSpec((None, None, d, tf), lambda i, j: (li, fi, 0, j)),
        pl.BlockSpec((None, None, tf, d), lambda i, j: (li, fi, j, 0)),
    ]
    args = [x, mod, g_pre.reshape(1, d), g_post.reshape(1, d), w_gate, w_up, w_down]
    aliases = {}
    if into is not None:
        assert into.shape == (out_rows, d)
        in_specs.append(pl.BlockSpec(memory_space=pl.ANY))
        aliases = {len(args): 0}
        args.append(into)
    return pl.pallas_call(
        _ffn_kernel if into is None else _ffn_kernel_into,
        grid=(n_rows // tm, f // tf),
        in_specs=in_specs,
        out_specs=pl.BlockSpec((tm, d), lambda i, j: (i + tile_offset, 0)),
        out_shape=jax.ShapeDtypeStruct((out_rows, d), F32),
        input_output_aliases=aliases,
        scratch_shapes=[pltpu.VMEM((tm, d), BF16), pltpu.VMEM((tm, d), F32), pltpu.VMEM((3, SUBLANES, d), F32)],
        compiler_params=_params("arbitrary", "arbitrary"),
        name="ffn_sublayer",
    )(*args)


def _in_proj_kernel(x_ref, mod_ref, gpre_ref, w_ref, o_ref, h_ref, row_s):
    @pl.when(pl.program_id(1) == 0)
    def _():
        _modulate_rows(x_ref, gpre_ref, mod_ref, row_s, h_ref)

    o_ref[...] = _dot(h_ref[...], w_ref[...])


def mixer_in_proj(x, mod, g_pre, w_in, seq_len):
    rows, d = x.shape
    n = w_in.shape[1]
    tm, tn = PROJ_ROW_TILE, PROJ_COL_TILE
    assert rows % tm == 0 and n % tn == 0 and seq_len % tm == 0
    return pl.pallas_call(
        _in_proj_kernel,
        grid=(rows // tm, n // tn),
        in_specs=[
            pl.BlockSpec((tm, d), lambda i, j: (i, 0)),
            _mod_spec(d, tm, seq_len, mod.shape[0]),
            pl.BlockSpec((1, d), lambda i, j: (0, 0)),
            pl.BlockSpec((d, tn), lambda i, j: (0, j)),
        ],
        out_specs=pl.BlockSpec((tm, tn), lambda i, j: (i, j)),
        out_shape=jax.ShapeDtypeStruct((rows, n), F32),
        scratch_shapes=[pltpu.VMEM((tm, d), BF16), pltpu.VMEM((3, SUBLANES, d), F32)],
        compiler_params=_params("arbitrary", "arbitrary"),
        name="mixer_in_proj",
    )(x, mod, g_pre.reshape(1, d), w_in)


def _lru_kernel(*refs, reverse, n_lat, final):
    if final:
        (r_ref, rp_ref, rn_ref, cw_ref, cb_ref, wa_ref, ba_ref, wx_ref, bx_ref, lam_ref,
         hf_ref, g_ref, o_ref, a_s, b_s, h_s, hs_s) = refs
    else:
        (r_ref, rp_ref, rn_ref, cw_ref, cb_ref, wa_ref, ba_ref, wx_ref, bx_ref, lam_ref,
         o_ref, a_s, b_s, h_s) = refs
        hs_s = o_ref
    s = pl.program_id(1)
    tb, w = r_ref.shape
    is_lat = s > 0
    k = (n_lat - s) if reverse else (s - 1)
    has_prev = jnp.logical_and(is_lat, k > 0)
    has_next = jnp.logical_and(is_lat, k < n_lat - 1)

    prev = jnp.where(has_prev, rp_ref[...], 0.0)
    nxt = jnp.where(has_next, rn_ref[...], 0.0)
    r = r_ref[...]
    sub = lax.broadcasted_iota(jnp.int32, (SUBLANES, w), 0)

    def shifted(offset):
        if offset == 0:
            return r
        y = pltpu.roll(r, (-offset) % tb, axis=0)
        if offset < 0:
            edge = jnp.where(sub < -offset, pltpu.roll(prev, (-offset) % SUBLANES, axis=0), y[:SUBLANES])
            return jnp.concatenate([edge, y[SUBLANES:]], axis=0)
        edge = jnp.where(sub >= SUBLANES - offset, pltpu.roll(nxt, (-offset) % SUBLANES, axis=0), y[tb - SUBLANES:])
        return jnp.concatenate([y[:tb - SUBLANES], edge], axis=0)

    u = shifted(-CONV_LEFT) * cw_ref[0:1, :]
    for t in range(1, CONV_W):
        u = u + shifted(t - CONV_LEFT) * cw_ref[t:t + 1, :]
    u = u + cb_ref[...]

    ub = u.astype(BF16)
    blk = w // LRU_BLOCKS
    ra = jnp.concatenate([_dot(ub[:, n * blk:(n + 1) * blk], wa_ref[n]) for n in range(LRU_BLOCKS)], axis=1)
    xa = jnp.concatenate([_dot(ub[:, n * blk:(n + 1) * blk], wx_ref[n]) for n in range(LRU_BLOCKS)], axis=1)
    r_gate = _sigmoid(ra + ba_ref[...])
    i_gate = _sigmoid(xa + bx_ref[...])
    lam = lam_ref[...]
    softplus_neg_lam = jnp.maximum(-lam, 0.0) + jnp.log1p(jnp.exp(-jnp.abs(lam)))
    log_a = -LRU_C * r_gate * softplus_neg_lam
    a = jnp.exp(log_a)
    one_minus_a2 = -jnp.tanh(log_a) * (a * a + 1.0)
    a_s[...] = a
    b_s[...] = jnp.sqrt(one_minus_a2) * (i_gate * u)

    @pl.when(s == 0)
    def _():
        h_s[...] = jnp.zeros_like(h_s)

    def body(t, h):
        tt = (tb - 1 - t) if reverse else t
        h = a_s[pl.ds(tt, 1), :] * h + b_s[pl.ds(tt, 1), :]
        hs_s[pl.ds(tt, 1), :] = h
        return h

    h_s[...] = lax.fori_loop(0, tb, body, h_s[...], unroll=8)

    if final:
        o_ref[...] = (jax.nn.gelu(g_ref[...]) * (hf_ref[...] + hs_s[...])).astype(o_ref.dtype)


def rglru_direction(proj, r_col, g_col, conv_w, conv_b, wa, ba, wx, bx, lam, h_fwd, *, batch, seq_len, ctx_len):
    rows = proj.shape[0]
    w = wa.shape[0] * wa.shape[1]
    tb = SEQ_BLOCK
    assert ctx_len == tb and seq_len % tb == 0
    n_lat = seq_len // tb
    ctx0 = batch * n_lat
    hb = tb // SUBLANES
    n_halo = rows // SUBLANES
    final = h_fwd is not None

    def blk(b, s):
        k = (n_lat - s) if final else (s - 1)
        return jnp.where(s == 0, ctx0 + b, b * n_lat + k)

    row_spec = lambda col: pl.BlockSpec((tb, w), lambda b, s: (blk(b, s), col))
    vec = lambda n: pl.BlockSpec((n, w), lambda b, s: (0, 0))
    mat = pl.BlockSpec(wa.shape, lambda b, s: (0, 0, 0))
    in_specs = [
        row_spec(r_col),
        pl.BlockSpec((SUBLANES, w), lambda b, s: (jnp.maximum(blk(b, s) * hb - 1, 0), r_col)),
        pl.BlockSpec((SUBLANES, w), lambda b, s: (jnp.minimum((blk(b, s) + 1) * hb, n_halo - 1), r_col)),
        vec(CONV_W), vec(1), mat, vec(1), mat, vec(1), vec(1),
    ]
    args = [proj, proj, proj, conv_w, conv_b.reshape(1, w), wa, ba.reshape(1, w), wx, bx.reshape(1, w),
            lam.reshape(1, w)]
    scratch = [pltpu.VMEM((tb, w), F32), pltpu.VMEM((tb, w), F32), pltpu.VMEM((1, w), F32)]
    if final:
        in_specs += [row_spec(0), row_spec(g_col)]
        args += [h_fwd, proj]
        scratch.append(pltpu.VMEM((tb, w), F32))
    return pl.pallas_call(
        functools.partial(_lru_kernel, reverse=final, n_lat=n_lat, final=final),
        grid=(batch, n_lat + 1),
        in_specs=in_specs,
        out_specs=row_spec(0),
        out_shape=jax.ShapeDtypeStruct((rows, w), BF16 if final else F32),
        scratch_shapes=scratch,
        compiler_params=_params("arbitrary", "arbitrary"),
        name="rglru_reverse" if final else "rglru_forward",
    )(*args)


def _rotate_halves(x, cos, sin):
    half = x.shape[-1] // 2
    x1, x2 = x[:, :half], x[:, half:]
    return jnp.concatenate([x1 * cos - x2 * sin, x1 * sin + x2 * cos], axis=-1)


def _retention_kernel(lg_ref, qf_ref, kf_ref, vf_ref, cf_ref, sf_ref, qb_ref, kb_ref, vb_ref, cb_ref, sb_ref,
                      of_ref, ob_ref, state, *, n_ctx):
    s = pl.program_id(1)
    c, width = qf_ref.shape
    dk = width // RET_HEADS
    is_lat = s >= n_ctx

    @pl.when(s == 0)
    def _():
        state[...] = jnp.zeros_like(state)

    ii = lax.broadcasted_iota(jnp.int32, (c, 1), 0).astype(F32)
    diff = ii - lax.broadcasted_iota(jnp.int32, (1, c), 1).astype(F32)
    k_scale = dk ** -0.5
    dirs = ((qf_ref, kf_ref, vf_ref, cf_ref, sf_ref, of_ref), (qb_ref, kb_ref, vb_ref, cb_ref, sb_ref, ob_ref))
    for d, (q_ref, k_ref, v_ref, cos_ref, sin_ref, o_ref) in enumerate(dirs):
        cos = jnp.where(is_lat, cos_ref[...], 1.0)
        sin = jnp.where(is_lat, sin_ref[...], 0.0)
        for h in range(RET_HEADS):
            lg = lg_ref[d, h]
            sl = slice(h * dk, (h + 1) * dk)
            q = _rotate_halves(q_ref[:, sl], cos, sin)
            k = _rotate_halves(k_ref[:, sl], cos, sin) * k_scale
            vb = v_ref[:, sl].astype(BF16)
            if d == 0:
                intra = jnp.where(diff >= 0, jnp.exp(lg * jnp.maximum(diff, 0.0)), 0.0)
                q_dec = jnp.exp(lg * (ii + 1.0))
                k_dec = jnp.exp(lg * (c - 1.0 - ii))
            else:
                intra = jnp.where(diff <= 0, jnp.exp(lg * jnp.maximum(-diff, 0.0)), 0.0)
                q_dec = jnp.exp(lg * (c - ii))
                k_dec = jnp.exp(lg * ii)
            s_dec = jnp.exp(lg * jnp.full((1, 1), float(c), F32))
            qb = q.astype(BF16)
            scores = lax.dot_general(qb, k.astype(BF16), (((1,), (1,)), ((), ())), preferred_element_type=F32)
            scores = scores * intra
            st = state[d, h]
            o = _dot(scores.astype(BF16), vb) + _dot((q * q_dec).astype(BF16), st.astype(BF16))
            o_ref[:, sl] = o
            kd_t = (k * k_dec).T.astype(BF16)
            state[d, h] = st * s_dec + _dot(kd_t, vb)


def retention_bidir(proj, q_col, k_col, v_col, log_g, cos, sin, *, batch, seq_len, ctx_len):
    rows = proj.shape[0]
    c = RET_CHUNK
    width = RET_HEADS * 2 * cos.shape[1]
    dk = width // RET_HEADS
    n_lat, n_ctx = seq_len // c, ctx_len // c
    ctx0 = batch * n_lat
    last = n_lat + n_ctx - 1

    def fwd(b, s):
        return jnp.where(s < n_ctx, ctx0 + b * n_ctx + s, b * n_lat + s - n_ctx)

    def bwd(b, s):
        return jnp.where(s < n_ctx, ctx0 + b * n_ctx + (n_ctx - 1 - s), b * n_lat + (last - s))

    def specs(idx):
        tab = pl.BlockSpec((c, dk // 2), lambda b, s: (jnp.where(s >= n_ctx, idx(b, s) - b * n_lat, 0), 0))
        return [pl.BlockSpec((c, width), lambda b, s, col=col: (idx(b, s), col)) for col in (q_col, k_col, v_col)] + [tab, tab]

    out_spec = lambda idx: pl.BlockSpec((c, width), lambda b, s: (idx(b, s), 0))
    return pl.pallas_call(
        functools.partial(_retention_kernel, n_ctx=n_ctx),
        grid=(batch, n_lat + n_ctx),
        in_specs=[pl.BlockSpec(memory_space=pltpu.SMEM)] + specs(fwd) + specs(bwd),
        out_specs=[out_spec(fwd), out_spec(bwd)],
        out_shape=[jax.ShapeDtypeStruct((rows, width), F32)] * 2,
        scratch_shapes=[pltpu.VMEM((2, RET_HEADS, dk, dk), F32)],
        compiler_params=_params("arbitrary", "arbitrary"),
        name="retention_bidir",
    )(log_g, proj, proj, proj, cos, sin, proj, proj, proj, cos, sin)


def _even_out_kernel(x_ref, mod_ref, gpost_ref, lru_ref, of_ref, ob_ref, gate_ref, gn_ref, w_ref, o_ref, y_s, row_s):
    o = of_ref[...] + ob_ref[...]
    width = o.shape[1]
    dv = width // RET_HEADS
    parts = []
    for h in range(RET_HEADS):
        oh = o[:, h * dv:(h + 1) * dv]
        mu = jnp.mean(oh, axis=-1, keepdims=True)
        cen = oh - mu
        var = jnp.mean(cen * cen, axis=-1, keepdims=True)
        parts.append(cen * lax.rsqrt(var + EPS))
    gate = gate_ref[...]
    ret = jnp.concatenate(parts, axis=-1) * gn_ref[...] * (gate * jax.nn.sigmoid(gate))
    lw = lru_ref.shape[1]
    y_s[...] = _dot(lru_ref[...], w_ref[0:lw, :]) + _dot(ret.astype(BF16), w_ref[lw:lw + width, :])
    _gated_residual_rows(x_ref, y_s, gpost_ref, mod_ref, row_s, o_ref, 1.0)


def even_mixer_out(x, mod, g_post, lru, o_fwd, o_bwd, proj, gate_col, gn_g, w_out, seq_len):
    rows, d = x.shape
    tm = OUT_ROW_TILE
    lw, rw = lru.shape[1], o_fwd.shape[1]
    return pl.pallas_call(
        _even_out_kernel,
        grid=(rows // tm,),
        in_specs=[
            pl.BlockSpec((tm, d), lambda i: (i, 0)),
            _mod_spec(d, tm, seq_len, mod.shape[0]),
            pl.BlockSpec((1, d), lambda i: (0, 0)),
            pl.BlockSpec((tm, lw), lambda i: (i, 0)),
            pl.BlockSpec((tm, rw), lambda i: (i, 0)),
            pl.BlockSpec((tm, rw), lambda i: (i, 0)),
            pl.BlockSpec((tm, rw), lambda i: (i, gate_col)),
            pl.BlockSpec((1, rw), lambda i: (0, 0)),
            pl.BlockSpec((lw + rw, d), lambda i: (0, 0)),
        ],
        out_specs=pl.BlockSpec((tm, d), lambda i: (i, 0)),
        out_shape=jax.ShapeDtypeStruct((rows, d), F32),
        scratch_shapes=[pltpu.VMEM((tm, d), F32), pltpu.VMEM((3, SUBLANES, d), F32)],
        compiler_params=_params("arbitrary"),
        name="even_mixer_out",
    )(x, mod, g_post.reshape(1, d), lru, o_fwd, o_bwd, proj, gn_g.reshape(1, rw), w_out)


def _head_norm_rotary(x, g, cos_full, sin_signed):
    ms = jnp.mean(x * x, axis=-1, keepdims=True)
    y = x * lax.rsqrt(ms + EPS) * g
    return y * cos_full + pltpu.roll(y, HEAD_DIM // 2, axis=1) * sin_signed


def _kv_prep_kernel(k_ref, v_ref, kg_ref, cos_ref, sin_ref, ko_ref, vo_ref):
    is_lat = pl.program_id(1) > 0
    cos = jnp.where(is_lat, cos_ref[...], 1.0)
    sin = jnp.where(is_lat, sin_ref[...], 0.0)
    for h in range(KV_HEADS):
        sl = slice(h * HEAD_DIM, (h + 1) * HEAD_DIM)
        ko_ref[:, sl] = _head_norm_rotary(k_ref[:, sl], kg_ref[...], cos, sin).astype(BF16)
        vo_ref[sl, :] = v_ref[:, sl].T.astype(BF16)


def kv_prepare(proj, k_col, v_col, k_g, cos_full, sin_signed, *, batch, seq_len, ctx_len):
    tb = SEQ_BLOCK
    kvw = KV_HEADS * HEAD_DIM
    assert ctx_len == tb
    n_lat = seq_len // tb
    ctx0 = batch * n_lat
    n_keys = seq_len + ctx_len
    src = lambda col: pl.BlockSpec((tb, kvw), lambda b, s: (jnp.where(s == 0, ctx0 + b, b * n_lat + s - 1), col))
    tab = pl.BlockSpec((tb, HEAD_DIM), lambda b, s: (jnp.maximum(s - 1, 0), 0))
    return pl.pallas_call(
        _kv_prep_kernel,
        grid=(batch, n_lat + 1),
        in_specs=[src(k_col), src(v_col), pl.BlockSpec((1, HEAD_DIM), lambda b, s: (0, 0)), tab, tab],
        out_specs=[pl.BlockSpec((tb, kvw), lambda b, s: (b * (n_lat + 1) + s, 0)),
                   pl.BlockSpec((kvw, tb), lambda b, s: (b, s))],
        out_shape=[jax.ShapeDtypeStruct((batch * n_keys, kvw), BF16),
                   jax.ShapeDtypeStruct((batch * kvw, n_keys), BF16)],
        compiler_params=_params("arbitrary", "arbitrary"),
        name="kv_prepare",
    )(proj, proj, k_g.reshape(1, HEAD_DIM), cos_full, sin_signed)


def _attention_kernel(q0_ref, q1_ref, q2_ref, qg_ref, cos_ref, sin_ref, k_ref, vt_ref, o_ref, s_ref):
    tq = q0_ref.shape[0]
    n_keys = k_ref.shape[0]
    tk = ATT_KV_TILE
    cos, sin = cos_ref[...], sin_ref[...]
    c = HEAD_DIM ** -0.5 * LOG2_E
    q_t = [(_head_norm_rotary(r[...], qg_ref[...], cos, sin) * c).T.astype(BF16) for r in (q0_ref, q1_ref, q2_ref)]

    n_chunks = n_keys // tk

    def scores(j, g):
        start = pl.multiple_of(j * tk, tk)
        return _dot(k_ref[pl.ds(start, tk), :], q_t[g])

    def update(j, s, state):
        m, l, acc = state
        start = pl.multiple_of(j * tk, tk)
        m_new = jnp.maximum(m, jnp.max(s, axis=0, keepdims=True))
        alpha = jnp.exp2(m - m_new)
        p = jnp.exp2(s - m_new)
        l = alpha * l + jnp.sum(p, axis=0, keepdims=True)
        acc = alpha * acc + _dot(vt_ref[:, pl.ds(start, tk)], p.astype(BF16))
        return m_new, l, acc

    s_ref[...] = scores(0, 0)

    def body(j, carry):
        state = list(carry)
        s_cur = s_ref[...]
        for g in range(GROUP):
            s_next = scores(j, g + 1) if g + 1 < GROUP else scores(jnp.minimum(j + 1, n_chunks - 1), 0)
            state[g] = update(j, s_cur, state[g])
            s_cur = s_next
        s_ref[...] = s_cur
        return tuple(state)

    init = tuple((jnp.full((1, tq), -jnp.inf, F32), jnp.zeros((1, tq), F32), jnp.zeros((HEAD_DIM, tq), F32))
                 for _ in range(GROUP))
    final = lax.fori_loop(0, n_chunks, body, init, unroll=True)
    for g, (_, l, acc) in enumerate(final):
        o_ref[:, g * HEAD_DIM:(g + 1) * HEAD_DIM] = (acc / l).T.astype(o_ref.dtype)


def gqa_attention(proj, q_col0, q_g, cos_full, sin_signed, keys, values_t, *, batch, seq_len, ctx_len):
    tq = ATT_Q_TILE
    n_keys = seq_len + ctx_len
    assert seq_len % tq == 0 and n_keys % ATT_KV_TILE == 0
    nq = seq_len // tq
    qspec = lambda g: pl.BlockSpec((tq, HEAD_DIM), lambda b, kh, i: (b * nq + i, q_col0 + kh * GROUP + g))
    tab = pl.BlockSpec((tq, HEAD_DIM), lambda b, kh, i: (i, 0))
    return pl.pallas_call(
        _attention_kernel,
        grid=(batch, KV_HEADS, nq),
        in_specs=[qspec(0), qspec(1), qspec(2), pl.BlockSpec((1, HEAD_DIM), lambda b, kh, i: (0, 0)), tab, tab,
                  pl.BlockSpec((n_keys, HEAD_DIM), lambda b, kh, i: (b, kh)),
                  pl.BlockSpec((HEAD_DIM, n_keys), lambda b, kh, i: (b * KV_HEADS + kh, 0))],
        out_specs=pl.BlockSpec((tq, GROUP * HEAD_DIM), lambda b, kh, i: (b * nq + i, kh)),
        out_shape=jax.ShapeDtypeStruct((batch * seq_len, ATT_HEADS * HEAD_DIM), BF16),
        scratch_shapes=[pltpu.VMEM((ATT_KV_TILE, tq), F32)],
        compiler_params=_params("arbitrary", "arbitrary", "arbitrary"),
        name="gqa_attention",
    )(proj, proj, proj, q_g.reshape(1, HEAD_DIM), cos_full, sin_signed, keys, values_t)


def _pool_kernel(x_ref, xp_ref, xn_ref, w_ref, scale_ref, o_ref, *, seq_len):
    i = pl.program_id(1)
    tb = x_ref.shape[0]
    x = x_ref[...]
    prev = jnp.where(i > 0, xp_ref[...], 0.0)
    nxt = jnp.where(i < pl.num_programs(1) - 1, xn_ref[...], 0.0)
    ext = jnp.concatenate([prev, x, nxt], axis=0)
    t = i * tb + lax.broadcasted_iota(jnp.int32, (tb, 1), 0)
    for gi, win in enumerate(POOL_WINDOWS):
        sl = slice(gi * POOL_GROUP, (gi + 1) * POOL_GROUP)
        half = win // 2
        e = ext[:, sl]
        tot = e[SUBLANES - half:SUBLANES - half + tb]
        for off in range(1 - half, half):
            tot = tot + e[SUBLANES + off:SUBLANES + off + tb]
        cnt = (jnp.minimum(t + half, seq_len) - jnp.maximum(t - half, 0)).astype(F32)
        centred = tot / cnt - x[:, sl]
        o_ref[:, sl] = (_dot(centred.astype(BF16), w_ref[gi]) * scale_ref[:, sl]).astype(o_ref.dtype)


def multiscale_pool(proj, pool_w, pool_scale, *, batch, seq_len):
    tb = POOL_BLOCK
    pw = POOL_GROUP * len(POOL_WINDOWS)
    nb = seq_len // tb
    hb = tb // SUBLANES
    assert max(POOL_WINDOWS) // 2 <= SUBLANES
    return pl.pallas_call(
        functools.partial(_pool_kernel, seq_len=seq_len),
        grid=(batch, nb),
        in_specs=[
            pl.BlockSpec((tb, pw), lambda b, i: (b * nb + i, 0)),
            pl.BlockSpec((SUBLANES, pw), lambda b, i: (jnp.maximum((b * nb + i) * hb - 1, 0), 0)),
            pl.BlockSpec((SUBLANES, pw), lambda b, i: ((b * nb + i + 1) * hb, 0)),
            pl.BlockSpec(pool_w.shape, lambda b, i: (0, 0, 0)),
            pl.BlockSpec((1, pw), lambda b, i: (0, 0)),
        ],
        out_specs=pl.BlockSpec((tb, pw), lambda b, i: (b * nb + i, 0)),
        out_shape=jax.ShapeDtypeStruct((batch * seq_len, pw), BF16),
        compiler_params=_params("arbitrary", "arbitrary"),
        name="multiscale_pool",
    )(proj, proj, proj, pool_w, pool_scale.reshape(1, pw))


def _odd_out_kernel(x_ref, mod_ref, gpost_ref, pool_ref, att_ref, w_ref, o_ref, y_s, row_s):
    pw = pool_ref.shape[1]
    y_s[...] = _dot(pool_ref[...], w_ref[0:pw, :]) + _dot(att_ref[...], w_ref[pw:, :])
    _gated_residual_rows(x_ref, y_s, gpost_ref, mod_ref, row_s, o_ref, 1.0)


def odd_mixer_out(x, n_rows, mod, g_post, pooled, att, w_out, seq_len):
    d = x.shape[1]
    tm = OUT_ROW_TILE
    pw, aw = pooled.shape[1], att.shape[1]
    return pl.pallas_call(
        _odd_out_kernel,
        grid=(n_rows // tm,),
        in_specs=[
            pl.BlockSpec((tm, d), lambda i: (i, 0)),
            _mod_spec(d, tm, seq_len, mod.shape[0]),
            pl.BlockSpec((1, d), lambda i: (0, 0)),
            pl.BlockSpec((tm, pw), lambda i: (i, 0)),
            pl.BlockSpec((tm, aw), lambda i: (i, 0)),
            pl.BlockSpec((pw + aw, d), lambda i: (0, 0)),
        ],
        out_specs=pl.BlockSpec((tm, d), lambda i: (i, 0)),
        out_shape=jax.ShapeDtypeStruct((n_rows, d), F32),
        scratch_shapes=[pltpu.VMEM((tm, d), F32), pltpu.VMEM((3, SUBLANES, d), F32)],
        compiler_params=_params("arbitrary"),
        name="odd_mixer_out",
    )(x, mod, g_post.reshape(1, d), pooled, att, w_out)


def kernel(x, c, ctx, c_ctx, mod_w, mod_b, norm_pre, norm_post, ffn_gate, ffn_up, ffn_down,
           ev_w_in, ev_w_out, lru_conv_w, lru_conv_b, lru_wa, lru_ba, lru_wx, lru_bx, lru_lambda,
           ret_decay_logit, ret_gn, od_w_in, od_w_out, pool_w, pool_scale, q_norm, k_norm):
    B, S, D = x.shape
    Lc = ctx.shape[1]
    depth = mod_w.shape[0]
    assert depth == 2 and B < SUBLANES
    geom = dict(batch=B, seq_len=S, ctx_len=Lc)
    n_lat_rows = B * S

    grid_rows = S // GRID_W
    row = jnp.repeat(jnp.arange(grid_rows, dtype=F32), GRID_W)
    col = jnp.tile(jnp.arange(GRID_W, dtype=F32), grid_rows)
    n_ax = HEAD_DIM // 4
    f_ax = ROPE_THETA ** (-jnp.arange(n_ax, dtype=F32) / n_ax)
    ang2 = jnp.concatenate([row[:, None] * f_ax, col[:, None] * f_ax], axis=-1)
    cos2, sin2 = jnp.cos(ang2), jnp.sin(ang2)
    cos_full = jnp.concatenate([cos2, cos2], axis=-1)
    sin_signed = jnp.concatenate([-sin2, sin2], axis=-1)
    ret_dk = ev_w_out.shape[1] // 2 // RET_HEADS
    n_r = ret_dk // 2
    f_r = RET_THETA ** (-jnp.arange(n_r, dtype=F32) / n_r)
    ang1 = jnp.arange(S, dtype=F32)[:, None] * f_r
    cos1, sin1 = jnp.cos(ang1), jnp.sin(ang1)

    wg, wu, wd = ffn_gate.astype(BF16), ffn_up.astype(BF16), ffn_down.astype(BF16)

    c_all = jnp.zeros((SUBLANES, D), F32).at[:B].set(c).at[B].set(c_ctx)
    mods = modulation_vectors(c_all, mod_w, mod_b).reshape(depth, SUBLANES, 3, 3, D)[:, :B + 1]

    n_ctx_rows = B * Lc
    n_rows = n_lat_rows + n_ctx_rows

    li, e = 0, 0
    mod = lambda sub: mods[li, :, sub]

    def ffn(rows_in, n, sub, fi, **kw):
        return ffn_sublayer(rows_in, n, mod(sub), norm_pre[li, sub], norm_post[li, sub], wg, wu, wd, (li, fi), S,
                            **kw)

    xs = ffn(x.reshape(n_lat_rows, D), n_lat_rows, 0, 0, out_rows=n_rows)
    xs = ffn(ctx.reshape(n_ctx_rows, D), n_ctx_rows, 0, 0, out_rows=n_rows, tile_offset=n_lat_rows // ROW_TILE,
             into=xs)
    proj = mixer_in_proj(xs, mod(1), norm_pre[li, 1], ev_w_in[e].astype(BF16), S)
    lru_args = lambda d: (lru_conv_w[e], lru_conv_b[e], lru_wa[e, d].astype(BF16), lru_ba[e, d],
                          lru_wx[e, d].astype(BF16), lru_bx[e, d], lru_lambda[e, d])
    h_fwd = rglru_direction(proj, 1, 0, *lru_args(0), None, **geom)
    lru = rglru_direction(proj, 1, 0, *lru_args(1), h_fwd, **geom)
    log_g = -jax.nn.softplus(-ret_decay_logit[e].astype(F32))
    o_fwd, o_bwd = retention_bidir(proj, 2, 3, 4, log_g, cos1, sin1, **geom)
    xs = even_mixer_out(xs, mod(1), norm_post[li, 1], lru, o_fwd, o_bwd, proj, 5, ret_gn[e],
                        ev_w_out[e].astype(BF16), S)
    xs = ffn(xs, n_rows, 2, 1)

    li, o = 1, 0
    xs = ffn(xs, n_rows, 0, 0)
    proj = mixer_in_proj(xs, mod(1), norm_pre[li, 1], od_w_in[o].astype(BF16), S)
    pw = POOL_GROUP * len(POOL_WINDOWS)
    kvw = KV_HEADS * HEAD_DIM
    q0 = pw // HEAD_DIM
    k_col = (pw + ATT_HEADS * HEAD_DIM) // kvw
    keys, values_t = kv_prepare(proj, k_col, k_col + 1, k_norm[o], cos_full, sin_signed, **geom)
    att = gqa_attention(proj, q0, q_norm[o], cos_full, sin_signed, keys, values_t, **geom)
    pooled = multiscale_pool(proj, pool_w[o].astype(BF16), pool_scale[o], batch=B, seq_len=S)
    xl = odd_mixer_out(xs, n_lat_rows, mod(1), norm_post[li, 1], pooled, att, od_w_out[o].astype(BF16), S)
    xl = ffn(xl, n_lat_rows, 2, 1)
    return xl.reshape(B, S, D)
```

```python
import functools

import jax
import jax.numpy as jnp
from jax import lax
from jax.experimental import pallas as pl
from jax.experimental.pallas import tpu as pltpu

GRID_W = 64
EPS = 1e-6
FFN_STEP = 0.5
LRU_BLOCKS = 8
LRU_C = 8.0
CONV_W = 4
CONV_LEFT = 2
RET_HEADS = 4
RET_CHUNK = 128
RET_THETA = 10000.0
POOL_WINDOWS = (2, 4, 8, 16)
POOL_GROUP = 128
ATT_HEADS = 12
KV_HEADS = 4
GROUP = ATT_HEADS // KV_HEADS
HEAD_DIM = 128
ROPE_THETA = 10000.0

VMEM_LIMIT_BYTES = 56 * 1024 * 1024
SUBLANES = 8

ROW_TILE = 512
PROJ_ROW_TILE = 1024
OUT_ROW_TILE = 256
FFN_FF_TILE = 512
PROJ_COL_TILE = 1536
SEQ_BLOCK = 256
ATT_Q_TILE = 512
ATT_KV_TILE = 1408
POOL_BLOCK = 512
LOG2_E = 1.4426950408889634
NORM_ROWS = 16
NORM_UNROLL = 16

BF16 = jnp.bfloat16
F32 = jnp.float32


def _params(*sem):
    return pltpu.CompilerParams(dimension_semantics=sem, vmem_limit_bytes=VMEM_LIMIT_BYTES)


def _dot(a, b):
    return jnp.dot(a, b, preferred_element_type=F32)


def _sigmoid(x):
    return 0.5 * jnp.tanh(0.5 * x) + 0.5


def _modulate_rows(x_ref, gpre_ref, mod_ref, row_s, h_ref):
    _set_modulate_rows(gpre_ref, mod_ref, row_s)

    def body(i, carry):
        _modulate_group(x_ref, row_s, h_ref, i * NORM_ROWS)
        return carry

    lax.fori_loop(0, x_ref.shape[0] // NORM_ROWS, body, 0, unroll=NORM_UNROLL)


def _set_modulate_rows(gpre_ref, mod_ref, row_s):
    d = gpre_ref.shape[1]
    row_s[0] = jnp.broadcast_to(gpre_ref[...] * (1.0 + mod_ref[0, 1:2, :]), (SUBLANES, d))
    row_s[1] = jnp.broadcast_to(mod_ref[0, 0:1, :], (SUBLANES, d))


def _modulate_group(x_ref, row_s, h_ref, start):
    d = x_ref.shape[1]
    r = pl.multiple_of(start, NORM_ROWS)
    x = x_ref[pl.ds(r, NORM_ROWS), :]
    ms = jnp.mean(x * x, axis=-1, keepdims=True)
    xn = (x * lax.rsqrt(ms + EPS)).reshape(NORM_ROWS // SUBLANES, SUBLANES, d)
    h = xn * row_s[0] + row_s[1]
    h_ref[pl.ds(r, NORM_ROWS), :] = h.reshape(NORM_ROWS, d).astype(h_ref.dtype)


def _gated_residual_rows(x_ref, y_ref, gpost_ref, mod_ref, row_s, o_ref, step):
    _set_residual_row(gpost_ref, mod_ref, row_s, step)

    def body(i, carry):
        _residual_group(x_ref, y_ref, row_s, o_ref, i * NORM_ROWS)
        return carry

    lax.fori_loop(0, x_ref.shape[0] // NORM_ROWS, body, 0, unroll=NORM_UNROLL)


def _set_residual_row(gpost_ref, mod_ref, row_s, step):
    d = gpost_ref.shape[1]
    row_s[2] = jnp.broadcast_to((step * mod_ref[0, 2:3, :]) * gpost_ref[...], (SUBLANES, d))


def _residual_group(x_ref, y_ref, row_s, o_ref, start):
    d = x_ref.shape[1]
    r = pl.multiple_of(start, NORM_ROWS)
    y = y_ref[pl.ds(r, NORM_ROWS), :]
    ms = jnp.mean(y * y, axis=-1, keepdims=True)
    yn = (y * lax.rsqrt(ms + EPS)).reshape(NORM_ROWS // SUBLANES, SUBLANES, d)
    o_ref[pl.ds(r, NORM_ROWS), :] = x_ref[pl.ds(r, NORM_ROWS), :] + (yn * row_s[2]).reshape(NORM_ROWS, d)


def _mod_spec(d, tm, seq_len, n_groups, tile_offset=0):
    return pl.BlockSpec(
        (1, 3, d), lambda i, *_: (jnp.minimum(((i + tile_offset) * tm) // seq_len, n_groups - 1), 0, 0))


def _mod_kernel(c_ref, w_ref, b_ref, o_ref):
    c = c_ref[...]
    sc = c * jax.nn.sigmoid(c)
    o_ref[0] = _dot(sc.astype(BF16), w_ref[0].astype(BF16)) + b_ref[0]


def modulation_vectors(c_all, mod_w, mod_b):
    depth, d, n = mod_w.shape
    tn = PROJ_COL_TILE
    return pl.pallas_call(
        _mod_kernel,
        grid=(depth, n // tn),
        in_specs=[
            pl.BlockSpec((SUBLANES, d), lambda l, j: (0, 0)),
            pl.BlockSpec((1, d, tn), lambda l, j: (l, 0, j)),
            pl.BlockSpec((1, 1, tn), lambda l, j: (l, 0, j)),
        ],
        out_specs=pl.BlockSpec((1, SUBLANES, tn), lambda l, j: (l, 0, j)),
        out_shape=jax.ShapeDtypeStruct((depth, SUBLANES, n), F32),
        compiler_params=_params("arbitrary", "arbitrary"),
        name="modulation_vectors",
    )(c_all, mod_w, mod_b.reshape(depth, 1, n))


def _ffn_kernel(x_ref, mod_ref, gpre_ref, gpost_ref, wg_ref, wu_ref, wd_ref, o_ref, h_ref, acc_ref, row_s):
    j = pl.program_id(1)

    @pl.when(j == 0)
    def _():
        _modulate_rows(x_ref, gpre_ref, mod_ref, row_s, h_ref)
        acc_ref[...] = jnp.zeros_like(acc_ref)

    h = h_ref[...]
    g = _dot(h, wg_ref[...])
    u = _dot(h, wu_ref[...])
    a = (g * jax.nn.sigmoid(g)) * u
    acc_ref[...] += _dot(a.astype(BF16), wd_ref[...])

    @pl.when(j == pl.num_programs(1) - 1)
    def _():
        _gated_residual_rows(x_ref, acc_ref, gpost_ref, mod_ref, row_s, o_ref, FFN_STEP)


def _ffn_kernel_into(x_ref, mod_ref, gpre_ref, gpost_ref, wg_ref, wu_ref, wd_ref, into_ref, *rest):
    del into_ref
    _ffn_kernel(x_ref, mod_ref, gpre_ref, gpost_ref, wg_ref, wu_ref, wd_ref, *rest)


def ffn_sublayer(x, n_rows, mod, g_pre, g_post, w_gate, w_up, w_down, which, seq_len,
                 out_rows=None, tile_offset=0, into=None):
    d = x.shape[1]
    f = w_gate.shape[-1]
    tm, tf = ROW_TILE, FFN_FF_TILE
    li, fi = which
    out_rows = n_rows if out_rows is None else out_rows
    assert n_rows % tm == 0 and f % tf == 0 and seq_len % tm == 0 and out_rows >= tile_offset * tm + n_rows
    in_specs = [
        pl.BlockSpec((tm, d), lambda i, j: (i, 0)),
        _mod_spec(d, tm, seq_len, mod.shape[0], tile_offset),
        pl.BlockSpec((1, d), lambda i, j: (0, 0)),
        pl.BlockSpec((1, d), lambda i, j: (0, 0)),
        pl.BlockSpec((None, None, d, tf), lambda i, j: (li, fi, 0, j)),
        pl.BlockSpec((None, None, d, tf), lambda i, j: (li, fi, 0, j)),
        pl.BlockSpec((None, None, tf, d), lambda i, j: (li, fi, j, 0)),
    ]
    args = [x, mod, g_pre.reshape(1, d), g_post.reshape(1, d), w_gate, w_up, w_down]
    aliases = {}
    if into is not None:
        assert into.shape == (out_rows, d)
        in_specs.append(pl.BlockSpec(memory_space=pl.ANY))
        aliases = {len(args): 0}
        args.append(into)
    return pl.pallas_call(
        _ffn_kernel if into is None else _ffn_kernel_into,
        grid=(n_rows // tm, f // tf),
        in_specs=in_specs,
        out_specs=pl.BlockSpec((tm, d), lambda i, j: (i + tile_offset, 0)),
        out_shape=jax.ShapeDtypeStruct((out_rows, d), F32),
        input_output_aliases=aliases,
        scratch_shapes=[pltpu.VMEM((tm, d), BF16), pltpu.VMEM((tm, d), F32), pltpu.VMEM((3, SUBLANES, d), F32)],
        compiler_params=_params("arbitrary", "arbitrary"),
        name="ffn_sublayer",
    )(*args)


def _in_proj_kernel(x_ref, mod_ref, gpre_ref, w_ref, o_ref, h_ref, row_s):
    @pl.when(pl.program_id(1) == 0)
    def _():
        _modulate_rows(x_ref, gpre_ref, mod_ref, row_s, h_ref)

    o_ref[...] = _dot(h_ref[...], w_ref[...])


def mixer_in_proj(x, mod, g_pre, w_in, seq_len):
    rows, d = x.shape
    n = w_in.shape[1]
    tm, tn = PROJ_ROW_TILE, PROJ_COL_TILE
    assert rows % tm == 0 and n % tn == 0 and seq_len % tm == 0
    return pl.pallas_call(
        _in_proj_kernel,
        grid=(rows // tm, n // tn),
        in_specs=[
            pl.BlockSpec((tm, d), lambda i, j: (i, 0)),
            _mod_spec(d, tm, seq_len, mod.shape[0]),
            pl.BlockSpec((1, d), lambda i, j: (0, 0)),
            pl.BlockSpec((d, tn), lambda i, j: (0, j)),
        ],
        out_specs=pl.BlockSpec((tm, tn), lambda i, j: (i, j)),
        out_shape=jax.ShapeDtypeStruct((rows, n), F32),
        scratch_shapes=[pltpu.VMEM((tm, d), BF16), pltpu.VMEM((3, SUBLANES, d), F32)],
        compiler_params=_params("arbitrary", "arbitrary"),
        name="mixer_in_proj",
    )(x, mod, g_pre.reshape(1, d), w_in)


def _lru_kernel(*refs, reverse, n_lat, final):
    if final:
        (r_ref, rp_ref, rn_ref, cw_ref, cb_ref, wa_ref, ba_ref, wx_ref, bx_ref, lam_ref,
         hf_ref, g_ref, o_ref, a_s, b_s, h_s, hs_s) = refs
    else:
        (r_ref, rp_ref, rn_ref, cw_ref, cb_ref, wa_ref, ba_ref, wx_ref, bx_ref, lam_ref,
         o_ref, a_s, b_s, h_s) = refs
        hs_s = o_ref
    s = pl.program_id(1)
    tb, w = r_ref.shape
    is_lat = s > 0
    k = (n_lat - s) if reverse else (s - 1)
    has_prev = jnp.logical_and(is_lat, k > 0)
    has_next = jnp.logical_and(is_lat, k < n_lat - 1)

    prev = jnp.where(has_prev, rp_ref[...], 0.0)
    nxt = jnp.where(has_next, rn_ref[...], 0.0)
    r = r_ref[...]
    sub = lax.broadcasted_iota(jnp.int32, (SUBLANES, w), 0)

    def shifted(offset):
        if offset == 0:
            return r
        y = pltpu.roll(r, (-offset) % tb, axis=0)
        if offset < 0:
            edge = jnp.where(sub < -offset, pltpu.roll(prev, (-offset) % SUBLANES, axis=0), y[:SUBLANES])
            return jnp.concatenate([edge, y[SUBLANES:]], axis=0)
        edge = jnp.where(sub >= SUBLANES - offset, pltpu.roll(nxt, (-offset) % SUBLANES, axis=0), y[tb - SUBLANES:])
        return jnp.concatenate([y[:tb - SUBLANES], edge], axis=0)

    u = shifted(-CONV_LEFT) * cw_ref[0:1, :]
    for t in range(1, CONV_W):
        u = u + shifted(t - CONV_LEFT) * cw_ref[t:t + 1, :]
    u = u + cb_ref[...]

    ub = u.astype(BF16)
    blk = w // LRU_BLOCKS
    ra = jnp.concatenate([_dot(ub[:, n * blk:(n + 1) * blk], wa_ref[n]) for n in range(LRU_BLOCKS)], axis=1)
    xa = jnp.concatenate([_dot(ub[:, n * blk:(n + 1) * blk], wx_ref[n]) for n in range(LRU_BLOCKS)], axis=1)
    r_gate = _sigmoid(ra + ba_ref[...])
    i_gate = _sigmoid(xa + bx_ref[...])
    lam = lam_ref[...]
    softplus_neg_lam = jnp.maximum(-lam, 0.0) + jnp.log1p(jnp.exp(-jnp.abs(lam)))
    log_a = -LRU_C * r_gate * softplus_neg_lam
    a = jnp.exp(log_a)
    one_minus_a2 = -jnp.tanh(log_a) * (a * a + 1.0)
    a_s[...] = a
    b_s[...] = jnp.sqrt(one_minus_a2) * (i_gate * u)

    @pl.when(s == 0)
    def _():
        h_s[...] = jnp.zeros_like(h_s)

    def body(t, h):
        tt = (tb - 1 - t) if reverse else t
        h = a_s[pl.ds(tt, 1), :] * h + b_s[pl.ds(tt, 1), :]
        hs_s[pl.ds(tt, 1), :] = h
        return h

    h_s[...] = lax.fori_loop(0, tb, body, h_s[...], unroll=8)

    if final:
        o_ref[...] = (jax.nn.gelu(g_ref[...]) * (hf_ref[...] + hs_s[...])).astype(o_ref.dtype)


def rglru_direction(proj, r_col, g_col, conv_w, conv_b, wa, ba, wx, bx, lam, h_fwd, *, batch, seq_len, ctx_len):
    rows = proj.shape[0]
    w = wa.shape[0] * wa.shape[1]
    tb = SEQ_BLOCK
    assert ctx_len == tb and seq_len % tb == 0
    n_lat = seq_len // tb
    ctx0 = batch * n_lat
    hb = tb // SUBLANES
    n_halo = rows // SUBLANES
    final = h_fwd is not None

    def blk(b, s):
        k = (n_lat - s) if final else (s - 1)
        return jnp.where(s == 0, ctx0 + b, b * n_lat + k)

    row_spec = lambda col: pl.BlockSpec((tb, w), lambda b, s: (blk(b, s), col))
    vec = lambda n: pl.BlockSpec((n, w), lambda b, s: (0, 0))
    mat = pl.BlockSpec(wa.shape, lambda b, s: (0, 0, 0))
    in_specs = [
        row_spec(r_col),
        pl.BlockSpec((SUBLANES, w), lambda b, s: (jnp.maximum(blk(b, s) * hb - 1, 0), r_col)),
        pl.BlockSpec((SUBLANES, w), lambda b, s: (jnp.minimum((blk(b, s) + 1) * hb, n_halo - 1), r_col)),
        vec(CONV_W), vec(1), mat, vec(1), mat, vec(1), vec(1),
    ]
    args = [proj, proj, proj, conv_w, conv_b.reshape(1, w), wa, ba.reshape(1, w), wx, bx.reshape(1, w),
            lam.reshape(1, w)]
    scratch = [pltpu.VMEM((tb, w), F32), pltpu.VMEM((tb, w), F32), pltpu.VMEM((1, w), F32)]
    if final:
        in_specs += [row_spec(0), row_spec(g_col)]
        args += [h_fwd, proj]
        scratch.append(pltpu.VMEM((tb, w), F32))
    return pl.pallas_call(
        functools.partial(_lru_kernel, reverse=final, n_lat=n_lat, final=final),
        grid=(batch, n_lat + 1),
        in_specs=in_specs,
        out_specs=row_spec(0),
        out_shape=jax.ShapeDtypeStruct((rows, w), BF16 if final else F32),
        scratch_shapes=scratch,
        compiler_params=_params("arbitrary", "arbitrary"),
        name="rglru_reverse" if final else "rglru_forward",
    )(*args)


def _rotate_halves(x, cos, sin):
    half = x.shape[-1] // 2
    x1, x2 = x[:, :half], x[:, half:]
    return jnp.concatenate([x1 * cos - x2 * sin, x1 * sin + x2 * cos], axis=-1)


def _retention_kernel(lg_ref, qf_ref, kf_ref, vf_ref, cf_ref, sf_ref, qb_ref, kb_ref, vb_ref, cb_ref, sb_ref,
                      of_ref, ob_ref, state, *, n_ctx):
    s = pl.program_id(1)
    c, width = qf_ref.shape
    dk = width // RET_HEADS
    is_lat = s >= n_ctx

    @pl.when(s == 0)
    def _():
        state[...] = jnp.zeros_like(state)

    ii = lax.broadcasted_iota(jnp.int32, (c, 1), 0).astype(F32)
    diff = ii - lax.broadcasted_iota(jnp.int32, (1, c), 1).astype(F32)
    k_scale = dk ** -0.5
    dirs = ((qf_ref, kf_ref, vf_ref, cf_ref, sf_ref, of_ref), (qb_ref, kb_ref, vb_ref, cb_ref, sb_ref, ob_ref))
    for d, (q_ref, k_ref, v_ref, cos_ref, sin_ref, o_ref) in enumerate(dirs):
        cos = jnp.where(is_lat, cos_ref[...], 1.0)
        sin = jnp.where(is_lat, sin_ref[...], 0.0)
        for h in range(RET_HEADS):
            lg = lg_ref[d, h]
            sl = slice(h * dk, (h + 1) * dk)
            q = _rotate_halves(q_ref[:, sl], cos, sin)
            k = _rotate_halves(k_ref[:, sl], cos, sin) * k_scale
            vb = v_ref[:, sl].astype(BF16)
            if d == 0:
                intra = jnp.where(diff >= 0, jnp.exp(lg * jnp.maximum(diff, 0.0)), 0.0)
                q_dec = jnp.exp(lg * (ii + 1.0))
                k_dec = jnp.exp(lg * (c - 1.0 - ii))
            else:
                intra = jnp.where(diff <= 0, jnp.exp(lg * jnp.maximum(-diff, 0.0)), 0.0)
                q_dec = jnp.exp(lg * (c - ii))
                k_dec = jnp.exp(lg * ii)
            s_dec = jnp.exp(lg * jnp.full((1, 1), float(c), F32))
            qb = q.astype(BF16)
            scores = lax.dot_general(qb, k.astype(BF16), (((1,), (1,)), ((), ())), preferred_element_type=F32)
            scores = scores * intra
            st = state[d, h]
            o = _dot(scores.astype(BF16), vb) + _dot((q * q_dec).astype(BF16), st.astype(BF16))
            o_ref[:, sl] = o
            kd_t = (k * k_dec).T.astype(BF16)
            state[d, h] = st * s_dec + _dot(kd_t, vb)


def retention_bidir(proj, q_col, k_col, v_col, log_g, cos, sin, *, batch, seq_len, ctx_len):
    rows = proj.shape[0]
    c = RET_CHUNK
    width = RET_HEADS * 2 * cos.shape[1]
    dk = width // RET_HEADS
    n_lat, n_ctx = seq_len // c, ctx_len // c
    ctx0 = batch * n_lat
    last = n_lat + n_ctx - 1

    def fwd(b, s):
        return jnp.where(s < n_ctx, ctx0 + b * n_ctx + s, b * n_lat + s - n_ctx)

    def bwd(b, s):
        return jnp.where(s < n_ctx, ctx0 + b * n_ctx + (n_ctx - 1 - s), b * n_lat + (last - s))

    def specs(idx):
        tab = pl.BlockSpec((c, dk // 2), lambda b, s: (jnp.where(s >= n_ctx, idx(b, s) - b * n_lat, 0), 0))
        return [pl.BlockSpec((c, width), lambda b, s, col=col: (idx(b, s), col)) for col in (q_col, k_col, v_col)] + [tab, tab]

    out_spec = lambda idx: pl.BlockSpec((c, width), lambda b, s: (idx(b, s), 0))
    return pl.pallas_call(
        functools.partial(_retention_kernel, n_ctx=n_ctx),
        grid=(batch, n_lat + n_ctx),
        in_specs=[pl.BlockSpec(memory_space=pltpu.SMEM)] + specs(fwd) + specs(bwd),
        out_specs=[out_spec(fwd), out_spec(bwd)],
        out_shape=[jax.ShapeDtypeStruct((rows, width), F32)] * 2,
        scratch_shapes=[pltpu.VMEM((2, RET_HEADS, dk, dk), F32)],
        compiler_params=_params("arbitrary", "arbitrary"),
        name="retention_bidir",
    )(log_g, proj, proj, proj, cos, sin, proj, proj, proj, cos, sin)


def _even_out_kernel(x_ref, mod_ref, gpost_ref, lru_ref, of_ref, ob_ref, gate_ref, gn_ref, w_ref, o_ref, y_s, row_s):
    o = of_ref[...] + ob_ref[...]
    width = o.shape[1]
    dv = width // RET_HEADS
    parts = []
    for h in range(RET_HEADS):
        oh = o[:, h * dv:(h + 1) * dv]
        mu = jnp.mean(oh, axis=-1, keepdims=True)
        cen = oh - mu
        var = jnp.mean(cen * cen, axis=-1, keepdims=True)
        parts.append(cen * lax.rsqrt(var + EPS))
    gate = gate_ref[...]
    ret = jnp.concatenate(parts, axis=-1) * gn_ref[...] * (gate * jax.nn.sigmoid(gate))
    lw = lru_ref.shape[1]
    y_s[...] = _dot(lru_ref[...], w_ref[0:lw, :]) + _dot(ret.astype(BF16), w_ref[lw:lw + width, :])
    _gated_residual_rows(x_ref, y_s, gpost_ref, mod_ref, row_s, o_ref, 1.0)


def even_mixer_out(x, mod, g_post, lru, o_fwd, o_bwd, proj, gate_col, gn_g, w_out, seq_len):
    rows, d = x.shape
    tm = OUT_ROW_TILE
    lw, rw = lru.shape[1], o_fwd.shape[1]
    return pl.pallas_call(
        _even_out_kernel,
        grid=(rows // tm,),
        in_specs=[
            pl.BlockSpec((tm, d), lambda i: (i, 0)),
            _mod_spec(d, tm, seq_len, mod.shape[0]),
            pl.BlockSpec((1, d), lambda i: (0, 0)),
            pl.BlockSpec((tm, lw), lambda i: (i, 0)),
            pl.BlockSpec((tm, rw), lambda i: (i, 0)),
            pl.BlockSpec((tm, rw), lambda i: (i, 0)),
            pl.BlockSpec((tm, rw), lambda i: (i, gate_col)),
            pl.BlockSpec((1, rw), lambda i: (0, 0)),
            pl.BlockSpec((lw + rw, d), lambda i: (0, 0)),
        ],
        out_specs=pl.BlockSpec((tm, d), lambda i: (i, 0)),
        out_shape=jax.ShapeDtypeStruct((rows, d), F32),
        scratch_shapes=[pltpu.VMEM((tm, d), F32), pltpu.VMEM((3, SUBLANES, d), F32)],
        compiler_params=_params("arbitrary"),
        name="even_mixer_out",
    )(x, mod, g_post.reshape(1, d), lru, o_fwd, o_bwd, proj, gn_g.reshape(1, rw), w_out)


def _head_norm_rotary(x, g, cos_full, sin_signed):
    ms = jnp.mean(x * x, axis=-1, keepdims=True)
    y = x * lax.rsqrt(ms + EPS) * g
    return y * cos_full + pltpu.roll(y, HEAD_DIM // 2, axis=1) * sin_signed


def _kv_prep_kernel(k_ref, v_ref, kg_ref, cos_ref, sin_ref, ko_ref, vo_ref):
    is_lat = pl.program_id(1) > 0
    cos = jnp.where(is_lat, cos_ref[...], 1.0)
    sin = jnp.where(is_lat, sin_ref[...], 0.0)
    for h in range(KV_HEADS):
        sl = slice(h * HEAD_DIM, (h + 1) * HEAD_DIM)
        ko_ref[:, sl] = _head_norm_rotary(k_ref[:, sl], kg_ref[...], cos, sin).astype(BF16)
        vo_ref[sl, :] = v_ref[:, sl].T.astype(BF16)


def kv_prepare(proj, k_col, v_col, k_g, cos_full, sin_signed, *, batch, seq_len, ctx_len):
    tb = SEQ_BLOCK
    kvw = KV_HEADS * HEAD_DIM
    assert ctx_len == tb
    n_lat = seq_len // tb
    ctx0 = batch * n_lat
    n_keys = seq_len + ctx_len
    src = lambda col: pl.BlockSpec((tb, kvw), lambda b, s: (jnp.where(s == 0, ctx0 + b, b * n_lat + s - 1), col))
    tab = pl.BlockSpec((tb, HEAD_DIM), lambda b, s: (jnp.maximum(s - 1, 0), 0))
    return pl.pallas_call(
        _kv_prep_kernel,
        grid=(batch, n_lat + 1),
        in_specs=[src(k_col), src(v_col), pl.BlockSpec((1, HEAD_DIM), lambda b, s: (0, 0)), tab, tab],
        out_specs=[pl.BlockSpec((tb, kvw), lambda b, s: (b * (n_lat + 1) + s, 0)),
                   pl.BlockSpec((kvw, tb), lambda b, s: (b, s))],
        out_shape=[jax.ShapeDtypeStruct((batch * n_keys, kvw), BF16),
                   jax.ShapeDtypeStruct((batch * kvw, n_keys), BF16)],
        compiler_params=_params("arbitrary", "arbitrary"),
        name="kv_prepare",
    )(proj, proj, k_g.reshape(1, HEAD_DIM), cos_full, sin_signed)


def _attention_kernel(*refs):
    q_refs, (qg_ref, cos_ref, sin_ref, k_ref, vt_ref, o_ref, s_ref) = refs[:-7], refs[-7:]
    n_heads = len(q_refs)
    tq = q_refs[0].shape[0]
    n_keys = k_ref.shape[0]
    tk = ATT_KV_TILE
    cos, sin = cos_ref[...], sin_ref[...]
    c = HEAD_DIM ** -0.5 * LOG2_E
    q_t = [(_head_norm_rotary(r[...], qg_ref[...], cos, sin) * c).T.astype(BF16) for r in q_refs]

    n_chunks = n_keys // tk

    def scores(j, g):
        start = pl.multiple_of(j * tk, tk)
        return _dot(k_ref[pl.ds(start, tk), :], q_t[g])

    def update(j, s, state):
        m, l, acc = state
        start = pl.multiple_of(j * tk, tk)
        half = tq // 2
        m_new, p_halves, l_halves = [], [], []
        for lanes in (slice(0, half), slice(half, tq)):
            s_h = s[:, lanes]
            m_h = jnp.maximum(m[:, lanes], jnp.max(s_h, axis=0, keepdims=True))
            p_h = jnp.exp2(s_h - m_h)
            m_new.append(m_h)
            l_halves.append(jnp.sum(p_h, axis=0, keepdims=True))
            p_halves.append(p_h.astype(BF16))
        m_new = jnp.concatenate(m_new, axis=1)
        alpha = jnp.exp2(m - m_new)
        l = alpha * l + jnp.concatenate(l_halves, axis=1)
        acc = alpha * acc + _dot(vt_ref[:, pl.ds(start, tk)], jnp.concatenate(p_halves, axis=1))
        return m_new, l, acc

    s_ref[...] = scores(0, 0)

    def body(j, carry):
        state = list(carry)
        s_cur = s_ref[...]
        for g in range(n_heads):
            s_next = scores(j, g + 1) if g + 1 < n_heads else scores(jnp.minimum(j + 1, n_chunks - 1), 0)
            state[g] = update(j, s_cur, state[g])
            s_cur = s_next
        s_ref[...] = s_cur
        return tuple(state)

    init = tuple((jnp.full((1, tq), -jnp.inf, F32), jnp.zeros((1, tq), F32), jnp.zeros((HEAD_DIM, tq), F32))
                 for _ in range(n_heads))
    final = lax.fori_loop(0, n_chunks, body, init, unroll=True)
    for g, (_, l, acc) in enumerate(final):
        o_ref[:, g * HEAD_DIM:(g + 1) * HEAD_DIM] = (acc / l).T.astype(o_ref.dtype)


def gqa_attention(proj, q_col0, q_g, cos_full, sin_signed, keys, values_t, *, batch, seq_len, ctx_len):
    tq = ATT_Q_TILE
    n_keys = seq_len + ctx_len
    assert seq_len % tq == 0 and n_keys % ATT_KV_TILE == 0
    nq = seq_len // tq
    qspec = lambda g: pl.BlockSpec((tq, HEAD_DIM), lambda b, kh, i: (b * nq + i, q_col0 + kh * GROUP + g))
    tab = pl.BlockSpec((tq, HEAD_DIM), lambda b, kh, i: (i, 0))
    return pl.pallas_call(
        _attention_kernel,
        grid=(batch, KV_HEADS, nq),
        in_specs=[qspec(0), qspec(1), qspec(2), pl.BlockSpec((1, HEAD_DIM), lambda b, kh, i: (0, 0)), tab, tab,
                  pl.BlockSpec((n_keys, HEAD_DIM), lambda b, kh, i: (b, kh)),
                  pl.BlockSpec((HEAD_DIM, n_keys), lambda b, kh, i: (b * KV_HEADS + kh, 0))],
        out_specs=pl.BlockSpec((tq, GROUP * HEAD_DIM), lambda b, kh, i: (b * nq + i, kh)),
        out_shape=jax.ShapeDtypeStruct((batch * seq_len, ATT_HEADS * HEAD_DIM), BF16),
        scratch_shapes=[pltpu.VMEM((ATT_KV_TILE, tq), F32)],
        compiler_params=_params("arbitrary", "arbitrary", "arbitrary"),
        name="gqa_attention",
    )(proj, proj, proj, q_g.reshape(1, HEAD_DIM), cos_full, sin_signed, keys, values_t)


def _pool_kernel(x_ref, xp_ref, xn_ref, w_ref, scale_ref, o_ref, *, seq_len):
    i = pl.program_id(1)
    tb = x_ref.shape[0]
    x = x_ref[...]
    prev = jnp.where(i > 0, xp_ref[...], 0.0)
    nxt = jnp.where(i < pl.num_programs(1) - 1, xn_ref[...], 0.0)
    ext = jnp.concatenate([prev, x, nxt], axis=0)
    t = i * tb + lax.broadcasted_iota(jnp.int32, (tb, 1), 0)
    for gi, win in enumerate(POOL_WINDOWS):
        sl = slice(gi * POOL_GROUP, (gi + 1) * POOL_GROUP)
        half = win // 2
        e = ext[:, sl]
        tot = e[SUBLANES - half:SUBLANES - half + tb]
        for off in range(1 - half, half):
            tot = tot + e[SUBLANES + off:SUBLANES + off + tb]
        cnt = (jnp.minimum(t + half, seq_len) - jnp.maximum(t - half, 0)).astype(F32)
        centred = tot / cnt - x[:, sl]
        o_ref[:, sl] = (_dot(centred.astype(BF16), w_ref[gi]) * scale_ref[:, sl]).astype(o_ref.dtype)


def multiscale_pool(proj, pool_w, pool_scale, *, batch, seq_len):
    tb = POOL_BLOCK
    pw = POOL_GROUP * len(POOL_WINDOWS)
    nb = seq_len // tb
    hb = tb // SUBLANES
    assert max(POOL_WINDOWS) // 2 <= SUBLANES
    return pl.pallas_call(
        functools.partial(_pool_kernel, seq_len=seq_len),
        grid=(batch, nb),
        in_specs=[
            pl.BlockSpec((tb, pw), lambda b, i: (b * nb + i, 0)),
            pl.BlockSpec((SUBLANES, pw), lambda b, i: (jnp.maximum((b * nb + i) * hb - 1, 0), 0)),
            pl.BlockSpec((SUBLANES, pw), lambda b, i: ((b * nb + i + 1) * hb, 0)),
            pl.BlockSpec(pool_w.shape, lambda b, i: (0, 0, 0)),
            pl.BlockSpec((1, pw), lambda b, i: (0, 0)),
        ],
        out_specs=pl.BlockSpec((tb, pw), lambda b, i: (b * nb + i, 0)),
        out_shape=jax.ShapeDtypeStruct((batch * seq_len, pw), BF16),
        compiler_params=_params("arbitrary", "arbitrary"),
        name="multiscale_pool",
    )(proj, proj, proj, pool_w, pool_scale.reshape(1, pw))


def _odd_out_kernel(x_ref, mod_ref, gpost_ref, pool_ref, att_ref, w_ref, o_ref, y_s, row_s):
    pw = pool_ref.shape[1]
    y_s[...] = _dot(pool_ref[...], w_ref[0:pw, :]) + _dot(att_ref[...], w_ref[pw:, :])
    _gated_residual_rows(x_ref, y_s, gpost_ref, mod_ref, row_s, o_ref, 1.0)


def odd_mixer_out(x, n_rows, mod, g_post, pooled, att, w_out, seq_len):
    d = x.shape[1]
    tm = OUT_ROW_TILE
    pw, aw = pooled.shape[1], att.shape[1]
    return pl.pallas_call(
        _odd_out_kernel,
        grid=(n_rows // tm,),
        in_specs=[
            pl.BlockSpec((tm, d), lambda i: (i, 0)),
            _mod_spec(d, tm, seq_len, mod.shape[0]),
            pl.BlockSpec((1, d), lambda i: (0, 0)),
            pl.BlockSpec((tm, pw), lambda i: (i, 0)),
            pl.BlockSpec((tm, aw), lambda i: (i, 0)),
            pl.BlockSpec((pw + aw, d), lambda i: (0, 0)),
        ],
        out_specs=pl.BlockSpec((tm, d), lambda i: (i, 0)),
        out_shape=jax.ShapeDtypeStruct((n_rows, d), F32),
        scratch_shapes=[pltpu.VMEM((tm, d), F32), pltpu.VMEM((3, SUBLANES, d), F32)],
        compiler_params=_params("arbitrary"),
        name="odd_mixer_out",
    )(x, mod, g_post.reshape(1, d), pooled, att, w_out)


def kernel(x, c, ctx, c_ctx, mod_w, mod_b, norm_pre, norm_post, ffn_gate, ffn_up, ffn_down,
           ev_w_in, ev_w_out, lru_conv_w, lru_conv_b, lru_wa, lru_ba, lru_wx, lru_bx, lru_lambda,
           ret_decay_logit, ret_gn, od_w_in, od_w_out, pool_w, pool_scale, q_norm, k_norm):
    B, S, D = x.shape
    Lc = ctx.shape[1]
    depth = mod_w.shape[0]
    assert depth == 2 and B < SUBLANES
    geom = dict(batch=B, seq_len=S, ctx_len=Lc)
    n_lat_rows = B * S

    grid_rows = S // GRID_W
    row = jnp.repeat(jnp.arange(grid_rows, dtype=F32), GRID_W)
    col = jnp.tile(jnp.arange(GRID_W, dtype=F32), grid_rows)
    n_ax = HEAD_DIM // 4
    f_ax = ROPE_THETA ** (-jnp.arange(n_ax, dtype=F32) / n_ax)
    ang2 = jnp.concatenate([row[:, None] * f_ax, col[:, None] * f_ax], axis=-1)
    cos2, sin2 = jnp.cos(ang2), jnp.sin(ang2)
    cos_full = jnp.concatenate([cos2, cos2], axis=-1)
    sin_signed = jnp.concatenate([-sin2, sin2], axis=-1)
    ret_dk = ev_w_out.shape[1] // 2 // RET_HEADS
    n_r = ret_dk // 2
    f_r = RET_THETA ** (-jnp.arange(n_r, dtype=F32) / n_r)
    ang1 = jnp.arange(S, dtype=F32)[:, None] * f_r
    cos1, sin1 = jnp.cos(ang1), jnp.sin(ang1)

    wg, wu, wd = ffn_gate.astype(BF16), ffn_up.astype(BF16), ffn_down.astype(BF16)

    c_all = jnp.zeros((SUBLANES, D), F32).at[:B].set(c).at[B].set(c_ctx)
    mods = modulation_vectors(c_all, mod_w, mod_b).reshape(depth, SUBLANES, 3, 3, D)[:, :B + 1]

    n_ctx_rows = B * Lc
    n_rows = n_lat_rows + n_ctx_rows

    li, e = 0, 0
    mod = lambda sub: mods[li, :, sub]

    def ffn(rows_in, n, sub, fi, **kw):
        return ffn_sublayer(rows_in, n, mod(sub), norm_pre[li, sub], norm_post[li, sub], wg, wu, wd, (li, fi), S,
                            **kw)

    xs = ffn(x.reshape(n_lat_rows, D), n_lat_rows, 0, 0, out_rows=n_rows)
    xs = ffn(ctx.reshape(n_ctx_rows, D), n_ctx_rows, 0, 0, out_rows=n_rows, tile_offset=n_lat_rows // ROW_TILE,
             into=xs)
    proj = mixer_in_proj(xs, mod(1), norm_pre[li, 1], ev_w_in[e].astype(BF16), S)
    lru_args = lambda d: (lru_conv_w[e], lru_conv_b[e], lru_wa[e, d].astype(BF16), lru_ba[e, d],
                          lru_wx[e, d].astype(BF16), lru_bx[e, d], lru_lambda[e, d])
    h_fwd = rglru_direction(proj, 1, 0, *lru_args(0), None, **geom)
    lru = rglru_direction(proj, 1, 0, *lru_args(1), h_fwd, **geom)
    log_g = -jax.nn.softplus(-ret_decay_logit[e].astype(F32))
    o_fwd, o_bwd = retention_bidir(proj, 2, 3, 4, log_g, cos1, sin1, **geom)
    xs = even_mixer_out(xs, mod(1), norm_post[li, 1], lru, o_fwd, o_bwd, proj, 5, ret_gn[e],
                        ev_w_out[e].astype(BF16), S)
    xs = ffn(xs, n_rows, 2, 1)

    li, o = 1, 0
    xs = ffn(xs, n_rows, 0, 0)
    proj = mixer_in_proj(xs, mod(1), norm_pre[li, 1], od_w_in[o].astype(BF16), S)
    pw = POOL_GROUP * len(POOL_WINDOWS)
    kvw = KV_HEADS * HEAD_DIM
    q0 = pw // HEAD_DIM
    k_col = (pw + ATT_HEADS * HEAD_DIM) // kvw
    keys, values_t = kv_prepare(proj, k_col, k_col + 1, k_norm[o], cos_full, sin_signed, **geom)
    att = gqa_attention(proj, q0, q_norm[o], cos_full, sin_signed, keys, values_t, **geom)
    pooled = multiscale_pool(proj, pool_w[o].astype(BF16), pool_scale[o], batch=B, seq_len=S)
    xl = odd_mixer_out(xs, n_lat_rows, mod(1), norm_post[li, 1], pooled, att, od_w_out[o].astype(BF16), S)
    xl = ffn(xl, n_lat_rows, 2, 1)
    return xl.reshape(B, S, D)
```
